```python
import math
import jax, jax.numpy as jnp
from jax import lax
import numpy as np

D_MODEL = 1024
BATCH = 16
SEQ = 4096
DEPTH = 1

D_RNN = 1024
N_RNN_BLOCKS = 16
RNN_BLOCK = D_RNN // N_RNN_BLOCKS
CONV_WIDTH = 4
LRU_C = 8.0
N_HEADS = 16
N_KV_HEADS = 4
HEAD_DIM = 64
ROT_DIM = HEAD_DIM // 4
ROPE_THETA = 500000.0
N_IDX_HEADS = 8
IDX_DIM = 64
IDX_ROT_DIM = IDX_DIM // 4
TOPK_MAX = 256
Q_BLOCK = 128
D_FF = -(-8 * D_MODEL // (3 * 256)) * 256
EPS = 1e-6

IN_SPLITS = (D_RNN, D_RNN, N_HEADS * HEAD_DIM, N_KV_HEADS * HEAD_DIM, N_KV_HEADS * HEAD_DIM,
             N_IDX_HEADS * IDX_DIM, IDX_DIM, N_IDX_HEADS, D_MODEL, D_MODEL)
D_IN = sum(IN_SPLITS)

kernel_name = "hybrid_rglru_dsa_gated_block"


def rms_norm(x, g):
    x32 = x.astype(jnp.float32)
    y = x32 * lax.rsqrt(jnp.mean(x32 * x32, axis=-1, keepdims=True) + EPS)
    return (y * g.astype(jnp.float32)).astype(x.dtype)


def layer_norm(x, g, b):
    x32 = x.astype(jnp.float32)
    mu = jnp.mean(x32, axis=-1, keepdims=True)
    var = jnp.mean(jnp.square(x32 - mu), axis=-1, keepdims=True)
    y = (x32 - mu) * lax.rsqrt(var + EPS)
    return (y * g.astype(jnp.float32) + b.astype(jnp.float32)).astype(x.dtype)


def partial_rope(x, rot_dim):
    s = x.shape[1]
    half = rot_dim // 2
    inv_freq = ROPE_THETA ** (-jnp.arange(half, dtype=jnp.float32) / half)
    ang = jnp.arange(s, dtype=jnp.float32)[:, None] * inv_freq[None, :]
    cos = jnp.cos(ang)[None, :, None, :]
    sin = jnp.sin(ang)[None, :, None, :]
    xr = x[..., :rot_dim].astype(jnp.float32)
    x1, x2 = xr[..., :half], xr[..., half:]
    rot = jnp.concatenate([x1 * cos - x2 * sin, x2 * cos + x1 * sin], axis=-1).astype(x.dtype)
    return jnp.concatenate([rot, x[..., rot_dim:]], axis=-1)


def causal_depthwise_conv(x, w, b):
    s = x.shape[1]
    xp = jnp.pad(x, ((0, 0), (CONV_WIDTH - 1, 0), (0, 0)))
    out = b
    for k in range(CONV_WIDTH):
        out = out + xp[:, k:k + s] * w[k]
    return out


def rg_lru(x, w_a, b_a, w_x, b_x, lam):
    bsz, s, _ = x.shape
    xb = x.reshape(bsz, s, N_RNN_BLOCKS, RNN_BLOCK)
    r = jax.nn.sigmoid(jnp.einsum('bshi,hij->bshj', xb, w_a).reshape(bsz, s, D_RNN) + b_a)
    i = jax.nn.sigmoid(jnp.einsum('bshi,hij->bshj', xb, w_x).reshape(bsz, s, D_RNN) + b_x)
    log_a = (-LRU_C * r.astype(jnp.float32)) * jax.nn.softplus(-lam.astype(jnp.float32))
    a = jnp.exp(log_a)
    u = jnp.sqrt(-jnp.expm1(2.0 * log_a)) * (i * x).astype(jnp.float32)

    def combine(left, right):
        a_l, b_l = left
        a_r, b_r = right
        return a_r * a_l, a_r * b_l + b_r

    _, h = lax.associative_scan(combine, (a, u), axis=1)
    return h.astype(x.dtype)


def dsa_sparse_attention(q, k, v, iq, ik, iw):
    bsz, s = q.shape[0], q.shape[1]
    topk = min(TOPK_MAX, s // 4)
    n_blk = s // Q_BLOCK
    rep = N_HEADS // N_KV_HEADS
    key_pos = jnp.arange(s)
    gather = jax.vmap(lambda t, idx: t[idx])

    def to_blocks(t):
        return jnp.moveaxis(t.reshape(bsz, n_blk, Q_BLOCK, *t.shape[2:]), 1, 0)

    def one_block(args):
        blk, qb, iqb, iwb = args
        q_pos = blk * Q_BLOCK + jnp.arange(Q_BLOCK)
        causal = key_pos[None, :] <= q_pos[:, None]
        logits = jnp.einsum('bqhd,bsd->bqhs', iqb, ik, preferred_element_type=jnp.float32)
        score = jnp.einsum('bqhs,bqh->bqs', jax.nn.relu(logits), iwb.astype(jnp.float32))
        score = jnp.where(causal[None], score, -jnp.inf)
        _, idx = lax.top_k(score, topk)
        valid = idx <= q_pos[None, :, None]
        k_sel = gather(k, idx)
        v_sel = gather(v, idx)
        qg = qb.reshape(bsz, Q_BLOCK, N_KV_HEADS, rep, HEAD_DIM)
        att = jnp.einsum('bqgrd,bqkgd->bqgrk', qg, k_sel,
                         preferred_element_type=jnp.float32) * (HEAD_DIM ** -0.5)
        att = jnp.where(valid[:, :, None, None, :], att, -jnp.inf)
        p = jax.nn.softmax(att, axis=-1).astype(v.dtype)
        o = jnp.einsum('bqgrk,bqkgd->bqgrd', p, v_sel)
        return o.reshape(bsz, Q_BLOCK, N_HEADS * HEAD_DIM)

    out = lax.map(one_block, (jnp.arange(n_blk), to_blocks(q), to_blocks(iq), to_blocks(iw)))
    return jnp.moveaxis(out, 0, 1).reshape(bsz, s, N_HEADS * HEAD_DIM)


def setup_inputs(seed: int = 0) -> dict:
    key = jax.random.key(seed)
    ks = jax.random.split(key, 24)
    f32 = jnp.float32

    def nrm(k, shape, fan_in):
        return jax.random.normal(k, shape, f32) * (fan_in ** -0.5)

    def gain(k, shape):
        return 1.0 + 0.05 * jax.random.normal(k, shape, f32)

    L = DEPTH
    u = jax.random.uniform(ks[9], (L, D_RNN), f32, 0.9, 0.999)
    s_lam = u ** (1.0 / LRU_C)
    rg_lambda = jnp.log(s_lam) - jnp.log1p(-s_lam)
    return {
        "x": jax.random.normal(ks[0], (BATCH, SEQ, D_MODEL), f32),
        "norm_mix_pre": gain(ks[1], (L, D_MODEL)),
        "w_in": nrm(ks[2], (L, D_MODEL, D_IN), D_MODEL),
        "conv_w": nrm(ks[3], (L, CONV_WIDTH, D_RNN), CONV_WIDTH),
        "conv_b": 0.02 * jax.random.normal(ks[4], (L, D_RNN), f32),
        "rg_w_a": nrm(ks[5], (L, N_RNN_BLOCKS, RNN_BLOCK, RNN_BLOCK), RNN_BLOCK),
        "rg_b_a": 0.02 * jax.random.normal(ks[6], (L, D_RNN), f32),
        "rg_w_x": nrm(ks[7], (L, N_RNN_BLOCKS, RNN_BLOCK, RNN_BLOCK), RNN_BLOCK),
        "rg_b_x": 0.02 * jax.random.normal(ks[8], (L, D_RNN), f32),
        "rg_lambda": rg_lambda,
        "idx_k_ln_g": gain(ks[10], (L, IDX_DIM)),
        "idx_k_ln_b": 0.02 * jax.random.normal(ks[11], (L, IDX_DIM), f32),
        "w_rnn_out": nrm(ks[12], (L, D_RNN, D_MODEL), D_RNN),
        "w_att_out": nrm(ks[13], (L, N_HEADS * HEAD_DIM, D_MODEL), N_HEADS * HEAD_DIM),
        "w_o": nrm(ks[14], (L, D_MODEL, D_MODEL), D_MODEL),
        "norm_mix_post": gain(ks[15], (L, D_MODEL)),
        "norm_ffn_pre": gain(ks[16], (L, D_MODEL)),
        "w_ffn_gate": nrm(ks[17], (L, D_MODEL, D_FF), D_MODEL),
        "w_ffn_up": nrm(ks[18], (L, D_MODEL, D_FF), D_MODEL),
        "w_ffn_down": nrm(ks[19], (L, D_FF, D_MODEL), D_FF),
        "norm_ffn_post": gain(ks[20], (L, D_MODEL)),
    }


def reference(x, norm_mix_pre, w_in, conv_w, conv_b, rg_w_a, rg_b_a, rg_w_x, rg_b_x, rg_lambda,
              idx_k_ln_g, idx_k_ln_b, w_rnn_out, w_att_out, w_o, norm_mix_post,
              norm_ffn_pre, w_ffn_gate, w_ffn_up, w_ffn_down, norm_ffn_post):
    bsz, s, _ = x.shape
    idx_w_scale = (N_IDX_HEADS ** -0.5) * (IDX_DIM ** -0.5)
    for l in range(DEPTH):
        h = rms_norm(x, norm_mix_pre[l])
        proj = h @ w_in[l]
        parts = []
        start = 0
        for width in IN_SPLITS:
            parts.append(proj[..., start:start + width])
            start += width
        xr, gr, q, k, v, iq, ik, iw, g_a, g_b = parts

        xc = causal_depthwise_conv(xr, conv_w[l], conv_b[l])
        y_rnn = rg_lru(xc, rg_w_a[l], rg_b_a[l], rg_w_x[l], rg_b_x[l], rg_lambda[l]) * jax.nn.gelu(gr)
        y_a = y_rnn @ w_rnn_out[l]

        q = partial_rope(q.reshape(bsz, s, N_HEADS, HEAD_DIM), ROT_DIM)
        k = partial_rope(k.reshape(bsz, s, N_KV_HEADS, HEAD_DIM), ROT_DIM)
        v = v.reshape(bsz, s, N_KV_HEADS, HEAD_DIM)
        iq = partial_rope(iq.reshape(bsz, s, N_IDX_HEADS, IDX_DIM), IDX_ROT_DIM)
        ik = layer_norm(ik, idx_k_ln_g[l], idx_k_ln_b[l])
        ik = partial_rope(ik[:, :, None, :], IDX_ROT_DIM)[:, :, 0, :]
        iw = iw * idx_w_scale
        y_att = dsa_sparse_attention(q, k, v, iq, ik, iw)
        y_b = y_att @ w_att_out[l]

        merged = jax.nn.sigmoid(g_a) * y_a + jax.nn.sigmoid(g_b) * y_b
        mix = merged @ w_o[l]
        x = x + rms_norm(mix, norm_mix_post[l])

        h = rms_norm(x, norm_ffn_pre[l])
        f = (jax.nn.silu(h @ w_ffn_gate[l]) * (h @ w_ffn_up[l])) @ w_ffn_down[l]
        x = x + rms_norm(f, norm_ffn_post[l])
    return x
```

```python
import functools
import math

import jax
import jax.numpy as jnp
from jax import lax
from jax.experimental import pallas as pl
from jax.experimental.pallas import tpu as pltpu

F32 = jnp.float32
I32 = jnp.int32
_MX = jnp.bfloat16

D_MODEL = 1024
D_RNN = 1024
N_RNN_BLOCKS = 16
RNN_BLOCK = D_RNN // N_RNN_BLOCKS
CONV_WIDTH = 4
LRU_C = 8.0
N_HEADS = 16
N_KV_HEADS = 4
HEAD_DIM = 64
ROT_DIM = HEAD_DIM // 4
ROPE_THETA = 500000.0
N_IDX_HEADS = 8
IDX_DIM = 64
TOPK_MAX = 256
D_FF = 2816
EPS = 1e-6

D_Q = N_HEADS * HEAD_DIM
D_KV = N_KV_HEADS * HEAD_DIM
D_QKV = D_Q + 2 * D_KV
D_IQ = N_IDX_HEADS * IDX_DIM
D_IDX = D_IQ + IDX_DIM + N_IDX_HEADS
D_IDX_PAD = 640
W_RNN, W_QKV, W_GATE = 2 * D_RNN, D_QKV, 2 * D_MODEL
W_ALL = W_RNN + W_QKV + W_GATE + D_IDX_PAD

LANES = 128
SUBLANES = 8
VMEM_LIMIT = 56 * 1024 * 1024
NEG_BIG = -1e30
INT_MIN = -(2 ** 31)

TM = 512
TS = 512
TQ = 128
CK = 512
GATE_GROUP = 256


def _const_spec(shape):
    nd = len(shape)
    return pl.BlockSpec(shape, lambda *_: (0,) * nd, pipeline_mode=pl.Buffered(1))


def _rms(x, g):
    return x * lax.rsqrt(jnp.mean(x * x, axis=-1, keepdims=True) + EPS) * g


def _inproj_kernel(x_ref, g_ref, w_ref, rnn_ref, qkv_ref, gate_ref, idx_ref):
    h = _rms(x_ref[...], g_ref[...]).astype(_MX)
    col = 0
    for ref, width in ((rnn_ref, W_RNN), (qkv_ref, W_QKV), (gate_ref, W_GATE), (idx_ref, D_IDX_PAD)):
        step = 512 if width % 512 == 0 else width
        for c in range(0, width, step):
            ref[:, c:c + step] = jnp.dot(
                h, w_ref[:, col + c:col + c + step], preferred_element_type=F32).astype(ref.dtype)
        col += width


def _inproj(x2, g, w_all):
    n = x2.shape[0]
    return pl.pallas_call(
        _inproj_kernel,
        grid=(n // TM,),
        in_specs=[pl.BlockSpec((TM, D_MODEL), lambda i: (i, 0)),
                  _const_spec((1, D_MODEL)),
                  _const_spec((D_MODEL, W_ALL))],
        out_specs=[pl.BlockSpec((TM, W_RNN), lambda i: (i, 0)),
                   pl.BlockSpec((TM, W_QKV), lambda i: (i, 0)),
                   pl.BlockSpec((TM, W_GATE), lambda i: (i, 0)),
                   pl.BlockSpec((TM, D_IDX_PAD), lambda i: (i, 0))],
        out_shape=[jax.ShapeDtypeStruct((n, W_RNN), _MX),
                   jax.ShapeDtypeStruct((n, W_QKV), _MX),
                   jax.ShapeDtypeStruct((n, W_GATE), _MX),
                   jax.ShapeDtypeStruct((n, D_IDX_PAD), F32)],
        compiler_params=pltpu.CompilerParams(
            dimension_semantics=("arbitrary",), vmem_limit_bytes=VMEM_LIMIT),
        name="inproj",
    )(x2, g, w_all)


def _rnn_kernel(rnn_ref, cw_ref, cb_ref, wg_ref, ba_ref, bx_ref, lam_ref, y_ref,
                xpad_ref, a_ref, h_ref, sp_ref, sh_ref, cin_ref, hc_ref):
    s = pl.program_id(1)
    ts = y_ref.shape[0]
    ng = ts // SUBLANES

    @pl.when(s == 0)
    def _():
        xpad_ref[0:SUBLANES, :] = jnp.zeros((SUBLANES, D_RNN), F32)
        hc_ref[...] = jnp.zeros_like(hc_ref)

    @pl.when(s != 0)
    def _():
        xpad_ref[0:SUBLANES, :] = xpad_ref[ts:ts + SUBLANES, :]

    xpad_ref[SUBLANES:SUBLANES + ts, :] = rnn_ref[:, 0:D_RNN].astype(F32)

    lam = lam_ref[...]
    nl = -lam
    softplus_nl = jnp.maximum(nl, 0.0) + jnp.log1p(jnp.exp(-jnp.abs(nl)))

    for c in range(0, D_RNN, GATE_GROUP):
        cs = slice(c, c + GATE_GROUP)
        xc = cb_ref[:, cs]
        for k in range(CONV_WIDTH):
            off = SUBLANES - (CONV_WIDTH - 1) + k
            xc = xc + xpad_ref[off:off + ts, cs] * cw_ref[k:k + 1, cs]
        gz = jnp.dot(xc.astype(_MX), wg_ref[c // GATE_GROUP], preferred_element_type=F32)
        r = jax.nn.sigmoid(gz[:, :GATE_GROUP] + ba_ref[:, cs])
        i = jax.nn.sigmoid(gz[:, GATE_GROUP:] + bx_ref[:, cs])
        log_a = (-LRU_C * r) * softplus_nl[:, cs]
        a = jnp.exp(log_a)
        u = jnp.sqrt(-jnp.tanh(log_a) * (a * a + 1.0)) * (i * xc)
        for j in range(GATE_GROUP // LANES):
            a_ref[c // LANES + j] = a[:, j * LANES:(j + 1) * LANES]
            h_ref[c // LANES + j] = u[:, j * LANES:(j + 1) * LANES]

    for j in range(D_RNN // LANES):
        ls = slice(j * LANES, (j + 1) * LANES)
        h = h_ref[j, pl.ds(0, ng, stride=SUBLANES), :]
        p = a_ref[j, pl.ds(0, ng, stride=SUBLANES), :]
        for i in range(1, SUBLANES):
            a_i = a_ref[j, pl.ds(i, ng, stride=SUBLANES), :]
            h = a_i * h + h_ref[j, pl.ds(i, ng, stride=SUBLANES), :]
            p = a_i * p
            h_ref[j, pl.ds(i, ng, stride=SUBLANES), :] = h
            a_ref[j, pl.ds(i, ng, stride=SUBLANES), :] = p
        sp_ref[:, ls] = p
        sh_ref[:, ls] = h

    def carry_body(g, carry):
        cin_ref[pl.ds(g, 1), :] = carry
        return sp_ref[pl.ds(g, 1), :] * carry + sh_ref[pl.ds(g, 1), :]

    hc_ref[0:1, :] = lax.fori_loop(0, ng, carry_body, hc_ref[0:1, :])

    def out_body(g, _):
        r0 = pl.multiple_of(g * SUBLANES, SUBLANES)
        rows = pl.ds(r0, SUBLANES)
        cin = cin_ref[pl.ds(g, 1), :]
        for j in range(D_RNN // LANES):
            cj = jnp.broadcast_to(cin[:, j * LANES:(j + 1) * LANES], (SUBLANES, LANES))
            h_ref[j, rows, :] = h_ref[j, rows, :] + a_ref[j, rows, :] * cj
        return 0

    lax.fori_loop(0, ng, out_body, 0)

    for j in range(D_RNN // LANES):
        gr = rnn_ref[:, D_RNN + j * LANES:D_RNN + (j + 1) * LANES].astype(F32)
        y_ref[:, j * LANES:(j + 1) * LANES] = (
            h_ref[j] * jax.nn.gelu(gr, approximate=True)).astype(y_ref.dtype)


def _rnn(rnn, conv_w, conv_b, w_gates, b_a, b_x, lam, bsz, seq):
    n = rnn.shape[0]
    ts = min(TS, seq)
    ns = seq // ts
    ng = ts // SUBLANES
    return pl.pallas_call(
        _rnn_kernel,
        grid=(bsz, ns),
        in_specs=[pl.BlockSpec((ts, W_RNN), lambda b, s: (b * ns + s, 0)),
                  _const_spec((CONV_WIDTH, D_RNN)),
                  _const_spec((1, D_RNN)),
                  _const_spec((D_RNN // GATE_GROUP, GATE_GROUP, 2 * GATE_GROUP)),
                  _const_spec((1, D_RNN)),
                  _const_spec((1, D_RNN)),
                  _const_spec((1, D_RNN))],
        out_specs=pl.BlockSpec((ts, D_RNN), lambda b, s: (b * ns + s, 0)),
        out_shape=jax.ShapeDtypeStruct((n, D_RNN), _MX),
        scratch_shapes=[pltpu.VMEM((ts + SUBLANES, D_RNN), F32),
                        pltpu.VMEM((D_RNN // LANES, ts, LANES), F32),
                        pltpu.VMEM((D_RNN // LANES, ts, LANES), F32),
                        pltpu.VMEM((ng, D_RNN), F32),
                        pltpu.VMEM((ng, D_RNN), F32),
                        pltpu.VMEM((ng, D_RNN), F32),
                        pltpu.VMEM((SUBLANES, D_RNN), F32)],
        compiler_params=pltpu.CompilerParams(
            dimension_semantics=("arbitrary", "arbitrary"), vmem_limit_bytes=VMEM_LIMIT),
        name="rnn",
    )(rnn, conv_w, conv_b, w_gates, b_a, b_x, lam)


def _rope_block(x, cos, sin_lo, sin_hi):
    w = x.shape[1]
    half = ROT_DIM // 2
    return x * cos + pltpu.roll(x, w - half, 1) * sin_lo + pltpu.roll(x, half, 1) * sin_hi


def _prep_kernel(qkv_ref, idx_ref, cos_ref, slo_ref, shi_ref, lng_ref, lnb_ref,
                 q_ref, k_ref, v_ref, iq_ref, ik_ref, iw_ref):
    cos, slo, shi = cos_ref[...], slo_ref[...], shi_ref[...]
    scale = HEAD_DIM ** -0.5
    heads_per_blk = LANES // HEAD_DIM
    for blk in range(D_Q // LANES):
        xb = qkv_ref[:, blk * LANES:(blk + 1) * LANES].astype(F32) * scale
        rb = _rope_block(xb, cos, slo, shi).astype(q_ref.dtype)
        for j in range(heads_per_blk):
            q_ref[blk * heads_per_blk + j] = rb[:, j * HEAD_DIM:(j + 1) * HEAD_DIM]
    for blk in range(D_KV // LANES):
        c0 = D_Q + blk * LANES
        rb = _rope_block(qkv_ref[:, c0:c0 + LANES].astype(F32), cos, slo, shi).astype(k_ref.dtype)
        for j in range(heads_per_blk):
            k_ref[blk * heads_per_blk + j] = rb[:, j * HEAD_DIM:(j + 1) * HEAD_DIM]
            v0 = D_Q + D_KV + blk * LANES + j * HEAD_DIM
            v_ref[blk * heads_per_blk + j] = qkv_ref[:, v0:v0 + HEAD_DIM]
    for blk in range(D_IQ // LANES):
        rb = _rope_block(idx_ref[:, blk * LANES:(blk + 1) * LANES], cos, slo, shi).astype(iq_ref.dtype)
        for j in range(heads_per_blk):
            iq_ref[blk * heads_per_blk + j] = rb[:, j * IDX_DIM:(j + 1) * IDX_DIM]
    tail = idx_ref[:, D_IQ:D_IQ + LANES]
    is_key = lax.broadcasted_iota(I32, (1, LANES), 1) < IDX_DIM
    mu = jnp.sum(jnp.where(is_key, tail, 0.0), axis=-1, keepdims=True) * (1.0 / IDX_DIM)
    cen = jnp.where(is_key, tail - mu, 0.0)
    var = jnp.sum(cen * cen, axis=-1, keepdims=True) * (1.0 / IDX_DIM)
    ik = cen * lax.rsqrt(var + EPS) * lng_ref[...] + lnb_ref[...]
    ik = _rope_block(ik, cos, slo, shi)
    ik_ref[...] = ik[:, :IDX_DIM].astype(ik_ref.dtype)
    iw_ref[...] = tail[:, IDX_DIM:IDX_DIM + N_IDX_HEADS] * ((N_IDX_HEADS ** -0.5) * (IDX_DIM ** -0.5))


def _prep(qkv, idx, cos, slo, shi, ln_g, ln_b, bsz, seq):
    tm = min(TM, seq)
    ns = seq // tm
    tok = lambda b, s: (b * ns + s, 0)
    tab = pl.BlockSpec((tm, LANES), lambda b, s: (s, 0))
    head_major = lambda nh: pl.BlockSpec((None, nh, tm, HEAD_DIM), lambda b, s: (b, 0, s, 0))
    return pl.pallas_call(
        _prep_kernel,
        grid=(bsz, ns),
        in_specs=[pl.BlockSpec((tm, W_QKV), tok),
                  pl.BlockSpec((tm, D_IDX_PAD), tok),
                  tab, tab, tab,
                  _const_spec((1, LANES)),
                  _const_spec((1, LANES))],
        out_specs=[head_major(N_HEADS), head_major(N_KV_HEADS), head_major(N_KV_HEADS),
                   head_major(N_IDX_HEADS),
                   pl.BlockSpec((None, tm, IDX_DIM), lambda b, s: (b, s, 0)),
                   pl.BlockSpec((None, tm, N_IDX_HEADS), lambda b, s: (b, s, 0))],
        out_shape=[jax.ShapeDtypeStruct((bsz, N_HEADS, seq, HEAD_DIM), _MX),
                   jax.ShapeDtypeStruct((bsz, N_KV_HEADS, seq, HEAD_DIM), _MX),
                   jax.ShapeDtypeStruct((bsz, N_KV_HEADS, seq, HEAD_DIM), _MX),
                   jax.ShapeDtypeStruct((bsz, N_IDX_HEADS, seq, IDX_DIM), _MX),
                   jax.ShapeDtypeStruct((bsz, seq, IDX_DIM), _MX),
                   jax.ShapeDtypeStruct((bsz, seq, N_IDX_HEADS), F32)],
        compiler_params=pltpu.CompilerParams(
            dimension_semantics=("arbitrary", "arbitrary"), vmem_limit_bytes=VMEM_LIMIT),
        name="prep",
    )(qkv, idx, cos, slo, shi, ln_g, ln_b)


_NT = (((1,), (1,)), ((), ()))


def _attn_kernel(iq_ref, ik_ref, iw_ref, q_ref, k_ref, v_ref, o_ref,
                 key_ref, bias_ref, oh_ref, jmax_ref, *, topk, ck):
    qt = pl.program_id(1)
    tq = o_ref.shape[0]
    seq = ik_ref.shape[0]
    nck = (qt * tq + tq + ck - 1) // ck
    row = qt * tq + lax.broadcasted_iota(I32, (tq, 1), 0)
    lane = lax.broadcasted_iota(I32, (1, ck), 1)
    iw = iw_ref[...]

    def score_body(c, _):
        k0 = pl.multiple_of(c * ck, ck)
        ikc = ik_ref[pl.ds(k0, ck), :]
        acc = jnp.zeros((tq, ck), F32)
        for h in range(N_IDX_HEADS):
            logit = lax.dot_general(iq_ref[h], ikc, _NT, preferred_element_type=F32)
            acc = acc + jnp.maximum(logit, 0.0) * iw[:, h:h + 1]
        acc = jnp.where(acc == 0.0, 0.0, acc)
        bits = pltpu.bitcast(acc, I32)
        key = bits ^ ((bits >> 31) & 0x7FFFFFFF)
        key_ref[:, pl.ds(k0, ck)] = jnp.where(k0 + lane <= row, key, INT_MIN)
        return 0

    lax.fori_loop(0, nck, score_body, 0)

    def count(pred):
        def body(c, acc):
            k0 = pl.multiple_of(c * ck, ck)
            hit = jnp.where(pred(key_ref[:, pl.ds(k0, ck)], k0), 1.0, 0.0)
            for j in range(ck // LANES):
                acc = acc + hit[:, j * LANES:(j + 1) * LANES]
            return acc
        acc = lax.fori_loop(0, nck, body, jnp.zeros((tq, LANES), F32))
        return jnp.sum(acc, axis=1, keepdims=True)

    kk = jnp.minimum(row + 1, topk).astype(F32)
    c_nonneg = count(lambda keys, k0: keys >= 0)
    thr0 = jnp.where(c_nonneg >= kk, 0, INT_MIN).astype(I32)

    def bit_body(i, thr):
        cand = thr + (1 << (30 - i))
        cnt = count(lambda keys, k0: keys >= cand)
        return jnp.where(cnt >= kk, cand, thr)

    thr = lax.fori_loop(0, 31, bit_body, thr0)

    n_ge = count(lambda keys, k0: keys >= thr)
    jmax_ref[...] = jnp.full((tq, LANES), seq, I32)

    @pl.when(jnp.max(n_ge - kk) > 0.0)
    def _():
        need = kk - count(lambda keys, k0: keys > thr)
        nbits = max(1, (seq - 1).bit_length())

        def jbit_body(i, x):
            cand = x + (1 << (nbits - 1 - i))
            below = count(lambda keys, k0: (keys == thr) & (k0 + lane < cand))
            return jnp.where(below < need, cand, x)

        x = lax.fori_loop(0, nbits, jbit_body, jnp.zeros((tq, 1), I32))
        jmax_ref[...] = jnp.broadcast_to(x, (tq, LANES))

    jmax = jmax_ref[:, 0:1]

    def bias_body(c, _):
        k0 = pl.multiple_of(c * ck, ck)
        keys = key_ref[:, pl.ds(k0, ck)]
        sel = (keys > thr) | ((keys == thr) & (k0 + lane <= jmax))
        bias_ref[:, pl.ds(k0, ck)] = jnp.where(sel, 0.0, NEG_BIG)
        return 0

    lax.fori_loop(0, nck, bias_body, 0)

    rep = N_HEADS // N_KV_HEADS

    def head_body(h, _):
        g = h // rep
        q = q_ref[h]

        def body(c, carry):
            m, l, acc = carry
            k0 = pl.multiple_of(c * ck, ck)
            s = lax.dot_general(q, k_ref[g, pl.ds(k0, ck), :], _NT, preferred_element_type=F32)
            s = s + bias_ref[:, pl.ds(k0, ck)]
            m_new = jnp.maximum(m, jnp.max(s, axis=1, keepdims=True))
            alpha = jnp.exp(m - m_new)
            p = jnp.exp(s - m_new)
            l = alpha * l + jnp.sum(p, axis=1, keepdims=True)
            acc = alpha * acc + jnp.dot(p.astype(_MX), v_ref[g, pl.ds(k0, ck), :],
                                        preferred_element_type=F32)
            return m_new, l, acc

        m, l, acc = lax.fori_loop(
            0, nck, body,
            (jnp.full((tq, 1), NEG_BIG, F32), jnp.zeros((tq, 1), F32), jnp.zeros((tq, HEAD_DIM), F32)))
        oh_ref[h] = acc / l
        return 0

    lax.fori_loop(0, N_HEADS, head_body, 0)
    for h in range(N_HEADS):
        o_ref[:, h * HEAD_DIM:(h + 1) * HEAD_DIM] = oh_ref[h].astype(o_ref.dtype)


def _attn(iq, ik, iw, q, k, v, bsz, seq):
    topk = min(TOPK_MAX, seq // 4)
    tq = min(TQ, seq)
    ck = min(CK, seq)
    nq = seq // tq
    whole = lambda nh: pl.BlockSpec((None, nh, seq, HEAD_DIM), lambda b, t: (b, 0, 0, 0))
    tile = lambda nh: pl.BlockSpec((None, nh, tq, HEAD_DIM), lambda b, t: (b, 0, t, 0))
    return pl.pallas_call(
        functools.partial(_attn_kernel, topk=topk, ck=ck),
        grid=(bsz, nq),
        in_specs=[tile(N_IDX_HEADS),
                  pl.BlockSpec((None, seq, IDX_DIM), lambda b, t: (b, 0, 0)),
                  pl.BlockSpec((None, tq, N_IDX_HEADS), lambda b, t: (b, t, 0)),
                  tile(N_HEADS), whole(N_KV_HEADS), whole(N_KV_HEADS)],
        out_specs=pl.BlockSpec((tq, D_Q), lambda b, t: (b * nq + t, 0)),
        out_shape=jax.ShapeDtypeStruct((bsz * seq, D_Q), _MX),
        scratch_shapes=[pltpu.VMEM((tq, seq), I32),
                        pltpu.VMEM((tq, seq), F32),
                        pltpu.VMEM((N_HEADS, tq, HEAD_DIM), F32),
                        pltpu.VMEM((tq, LANES), I32)],
        compiler_params=pltpu.CompilerParams(
            dimension_semantics=("arbitrary", "arbitrary"), vmem_limit_bytes=VMEM_LIMIT),
        name="attn",
    )(iq, ik, iw, q, k, v)


def _merge_kernel(x_ref, yr_ref, ya_ref, gate_ref, wr_ref, wa_ref, wo_ref, g_ref, o_ref):
    y_a = jnp.dot(yr_ref[...], wr_ref[...], preferred_element_type=F32)
    y_b = jnp.dot(ya_ref[...], wa_ref[...], preferred_element_type=F32)
    g_a = jax.nn.sigmoid(gate_ref[:, :D_MODEL].astype(F32))
    g_b = jax.nn.sigmoid(gate_ref[:, D_MODEL:].astype(F32))
    merged = g_a * y_a + g_b * y_b
    mix = jnp.dot(merged.astype(_MX), wo_ref[...], preferred_element_type=F32)
    o_ref[...] = x_ref[...] + _rms(mix, g_ref[...])


def _merge(x2, y_rnn, y_att, gates, w_rnn_out, w_att_out, w_o, g):
    n = x2.shape[0]
    row = lambda w: pl.BlockSpec((TM, w), lambda i: (i, 0))
    return pl.pallas_call(
        _merge_kernel,
        grid=(n // TM,),
        in_specs=[row(D_MODEL), row(D_RNN), row(D_Q), row(W_GATE),
                  _const_spec((D_RNN, D_MODEL)), _const_spec((D_Q, D_MODEL)),
                  _const_spec((D_MODEL, D_MODEL)), _const_spec((1, D_MODEL))],
        out_specs=row(D_MODEL),
        out_shape=jax.ShapeDtypeStruct((n, D_MODEL), F32),
        compiler_params=pltpu.CompilerParams(
            dimension_semantics=("arbitrary",), vmem_limit_bytes=VMEM_LIMIT),
        name="merge",
    )(x2, y_rnn, y_att, gates, w_rnn_out, w_att_out, w_o, g)


FF_CHUNK = D_FF // 2


def _ffn_kernel(x_ref, gpre_ref, wg_ref, wu_ref, wd_ref, gpost_ref, o_ref):
    x = x_ref[...]
    h = _rms(x, gpre_ref[...]).astype(_MX)
    f = jnp.zeros(x.shape, F32)
    for c in range(0, D_FF, FF_CHUNK):
        gate = jnp.dot(h, wg_ref[:, c:c + FF_CHUNK], preferred_element_type=F32)
        up = jnp.dot(h, wu_ref[:, c:c + FF_CHUNK], preferred_element_type=F32)
        act = (jax.nn.silu(gate) * up).astype(_MX)
        f = f + jnp.dot(act, wd_ref[c:c + FF_CHUNK, :], preferred_element_type=F32)
    o_ref[...] = x + _rms(f, gpost_ref[...])


def _ffn(x1, g_pre, w_gate, w_up, w_down, g_post):
    n = x1.shape[0]
    row = pl.BlockSpec((TM, D_MODEL), lambda i: (i, 0))
    return pl.pallas_call(
        _ffn_kernel,
        grid=(n // TM,),
        in_specs=[row, _const_spec((1, D_MODEL)),
                  _const_spec((D_MODEL, D_FF)), _const_spec((D_MODEL, D_FF)),
                  _const_spec((D_FF, D_MODEL)), _const_spec((1, D_MODEL))],
        out_specs=row,
        out_shape=jax.ShapeDtypeStruct((n, D_MODEL), F32),
        compiler_params=pltpu.CompilerParams(
            dimension_semantics=("arbitrary",), vmem_limit_bytes=VMEM_LIMIT),
        name="ffn",
    )(x1, g_pre, w_gate, w_up, w_down, g_post)


def _rope_tables(seq):
    half = ROT_DIM // 2
    inv_freq = ROPE_THETA ** (-jnp.arange(half, dtype=F32) / half)
    ang = jnp.arange(seq, dtype=F32)[:, None] * inv_freq[None, :]
    cos, sin = jnp.cos(ang), jnp.sin(ang)
    rest = HEAD_DIM - ROT_DIM
    ones, zeros, zh = jnp.ones((seq, rest), F32), jnp.zeros((seq, rest), F32), jnp.zeros((seq, half), F32)
    cos_t = jnp.concatenate([cos, cos, ones], axis=1)
    slo_t = jnp.concatenate([-sin, zh, zeros], axis=1)
    shi_t = jnp.concatenate([zh, sin, zeros], axis=1)
    two = lambda t: jnp.concatenate([t, t], axis=1)
    return two(cos_t), two(slo_t), two(shi_t)


def _gate_weights(w_a, w_x):
    per = GATE_GROUP // RNN_BLOCK

    def diag(w):
        w = w.reshape(D_RNN // GATE_GROUP, per, RNN_BLOCK, RNN_BLOCK)
        eye = jnp.eye(per, dtype=w.dtype)
        return jnp.einsum('gpij,pq->gpiqj', w, eye).reshape(-1, GATE_GROUP, GATE_GROUP)

    return jnp.concatenate([diag(w_a), diag(w_x)], axis=-1).astype(_MX)


def kernel(x, norm_mix_pre, w_in, conv_w, conv_b, rg_w_a, rg_b_a, rg_w_x, rg_b_x, rg_lambda,
           idx_k_ln_g, idx_k_ln_b, w_rnn_out, w_att_out, w_o, norm_mix_post,
           norm_ffn_pre, w_ffn_gate, w_ffn_up, w_ffn_down, norm_ffn_post):
    bsz, seq, _ = x.shape
    depth = w_in.shape[0]
    x2 = x.reshape(bsz * seq, D_MODEL)
    cos, slo, shi = _rope_tables(seq)
    o_idx = W_RNN + W_QKV
    for l in range(depth):
        w = w_in[l]
        w_all = jnp.concatenate(
            [w[:, :o_idx], w[:, o_idx + D_IDX:], w[:, o_idx:o_idx + D_IDX],
             jnp.zeros((D_MODEL, D_IDX_PAD - D_IDX), w.dtype)], axis=1).astype(_MX)
        rnn, qkv, gates, idx = _inproj(x2, norm_mix_pre[l][None], w_all)
        y_rnn = _rnn(rnn, conv_w[l], conv_b[l][None], _gate_weights(rg_w_a[l], rg_w_x[l]),
                     rg_b_a[l][None], rg_b_x[l][None], rg_lambda[l][None], bsz, seq)
        pad_lanes = lambda t: jnp.pad(t, (0, LANES - IDX_DIM))[None]
        q, k, v, iq, ik, iw = _prep(qkv, idx, cos, slo, shi,
                                    pad_lanes(idx_k_ln_g[l]), pad_lanes(idx_k_ln_b[l]), bsz, seq)
        y_att = _attn(iq, ik, iw, q, k, v, bsz, seq)
        x1 = _merge(x2, y_rnn, y_att, gates, w_rnn_out[l].astype(_MX), w_att_out[l].astype(_MX),
                    w_o[l].astype(_MX), norm_mix_post[l][None])
        x2 = _ffn(x1, norm_ffn_pre[l][None], w_ffn_gate[l].astype(_MX), w_ffn_up[l].astype(_MX),
                  w_ffn_down[l].astype(_MX), norm_ffn_post[l][None])
    return x2.reshape(bsz, seq, D_MODEL)
```

```python
import functools
import math

import jax
import jax.numpy as jnp
from jax import lax
from jax.experimental import pallas as pl
from jax.experimental.pallas import tpu as pltpu

F32 = jnp.float32
I32 = jnp.int32
_MX = jnp.bfloat16

D_MODEL = 1024
D_RNN = 1024
N_RNN_BLOCKS = 16
RNN_BLOCK = D_RNN // N_RNN_BLOCKS
CONV_WIDTH = 4
LRU_C = 8.0
N_HEADS = 16
N_KV_HEADS = 4
HEAD_DIM = 64
ROT_DIM = HEAD_DIM // 4
ROPE_THETA = 500000.0
N_IDX_HEADS = 8
IDX_DIM = 64
TOPK_MAX = 256
D_FF = 2816
EPS = 1e-6

D_Q = N_HEADS * HEAD_DIM
D_KV = N_KV_HEADS * HEAD_DIM
D_QKV = D_Q + 2 * D_KV
D_IQ = N_IDX_HEADS * IDX_DIM
D_IDX = D_IQ + IDX_DIM + N_IDX_HEADS
D_IDX_PAD = 640
W_RNN, W_QKV, W_GATE = 2 * D_RNN, D_QKV, 2 * D_MODEL
W_ALL = W_RNN + W_QKV + W_GATE + D_IDX_PAD

LANES = 128
SUBLANES = 8
VMEM_LIMIT = 56 * 1024 * 1024
NEG_BIG = -1e30
INT_MIN = -(2 ** 31)

TM = 512
TS = 512
TQ = 128
CK = 512
GATE_GROUP = 256


def _const_spec(shape):
    nd = len(shape)
    return pl.BlockSpec(shape, lambda *_: (0,) * nd, pipeline_mode=pl.Buffered(1))


def _rms(x, g):
    return x * lax.rsqrt(jnp.mean(x * x, axis=-1, keepdims=True) + EPS) * g


def _inproj_kernel(x_ref, g_ref, w_ref, rnn_ref, qkv_ref, gate_ref, idx_ref):
    h = _rms(x_ref[...], g_ref[...]).astype(_MX)
    col = 0
    for ref, width in ((rnn_ref, W_RNN), (qkv_ref, W_QKV), (gate_ref, W_GATE), (idx_ref, D_IDX_PAD)):
        step = 512 if width % 512 == 0 else width
        for c in range(0, width, step):
            ref[:, c:c + step] = jnp.dot(
                h, w_ref[:, col + c:col + c + step], preferred_element_type=F32).astype(ref.dtype)
        col += width


def _inproj(x2, g, w_all):
    n = x2.shape[0]
    return pl.pallas_call(
        _inproj_kernel,
        grid=(n // TM,),
        in_specs=[pl.BlockSpec((TM, D_MODEL), lambda i: (i, 0)),
                  _const_spec((1, D_MODEL)),
                  _const_spec((D_MODEL, W_ALL))],
        out_specs=[pl.BlockSpec((TM, W_RNN), lambda i: (i, 0)),
                   pl.BlockSpec((TM, W_QKV), lambda i: (i, 0)),
                   pl.BlockSpec((TM, W_GATE), lambda i: (i, 0)),
                   pl.BlockSpec((TM, D_IDX_PAD), lambda i: (i, 0))],
        out_shape=[jax.ShapeDtypeStruct((n, W_RNN), _MX),
                   jax.ShapeDtypeStruct((n, W_QKV), _MX),
                   jax.ShapeDtypeStruct((n, W_GATE), _MX),
                   jax.ShapeDtypeStruct((n, D_IDX_PAD), F32)],
        compiler_params=pltpu.CompilerParams(
            dimension_semantics=("arbitrary",), vmem_limit_bytes=VMEM_LIMIT),
        name="inproj",
    )(x2, g, w_all)


def _rnn_kernel(rnn_ref, cw_ref, cb_ref, wg_ref, ba_ref, bx_ref, lam_ref, y_ref,
                xpad_ref, a_ref, h_ref, sp_ref, sh_ref, cin_ref, hc_ref):
    s = pl.program_id(1)
    ts = y_ref.shape[0]
    ng = ts // SUBLANES

    @pl.when(s == 0)
    def _():
        xpad_ref[0:SUBLANES, :] = jnp.zeros((SUBLANES, D_RNN), F32)
        hc_ref[...] = jnp.zeros_like(hc_ref)

    @pl.when(s != 0)
    def _():
        xpad_ref[0:SUBLANES, :] = xpad_ref[ts:ts + SUBLANES, :]

    xpad_ref[SUBLANES:SUBLANES + ts, :] = rnn_ref[:, 0:D_RNN].astype(F32)

    lam = lam_ref[...]
    nl = -lam
    softplus_nl = jnp.maximum(nl, 0.0) + jnp.log1p(jnp.exp(-jnp.abs(nl)))

    for c in range(0, D_RNN, GATE_GROUP):
        cs = slice(c, c + GATE_GROUP)
        xc = cb_ref[:, cs]
        for k in range(CONV_WIDTH):
            off = SUBLANES - (CONV_WIDTH - 1) + k
            xc = xc + xpad_ref[off:off + ts, cs] * cw_ref[k:k + 1, cs]
        gz = jnp.dot(xc.astype(_MX), wg_ref[c // GATE_GROUP], preferred_element_type=F32)
        r = jax.nn.sigmoid(gz[:, :GATE_GROUP] + ba_ref[:, cs])
        i = jax.nn.sigmoid(gz[:, GATE_GROUP:] + bx_ref[:, cs])
        log_a = (-LRU_C * r) * softplus_nl[:, cs]
        a = jnp.exp(log_a)
        u = jnp.sqrt(-jnp.tanh(log_a) * (a * a + 1.0)) * (i * xc)
        for j in range(GATE_GROUP // LANES):
            a_ref[c // LANES + j] = a[:, j * LANES:(j + 1) * LANES]
            h_ref[c // LANES + j] = u[:, j * LANES:(j + 1) * LANES]

    for j in range(D_RNN // LANES):
        ls = slice(j * LANES, (j + 1) * LANES)
        h = h_ref[j, pl.ds(0, ng, stride=SUBLANES), :]
        p = a_ref[j, pl.ds(0, ng, stride=SUBLANES), :]
        for i in range(1, SUBLANES):
            a_i = a_ref[j, pl.ds(i, ng, stride=SUBLANES), :]
            h = a_i * h + h_ref[j, pl.ds(i, ng, stride=SUBLANES), :]
            p = a_i * p
            h_ref[j, pl.ds(i, ng, stride=SUBLANES), :] = h
            a_ref[j, pl.ds(i, ng, stride=SUBLANES), :] = p
        sp_ref[:, ls] = p
        sh_ref[:, ls] = h

    def carry_body(g, carry):
        cin_ref[pl.ds(g, 1), :] = carry
        return sp_ref[pl.ds(g, 1), :] * carry + sh_ref[pl.ds(g, 1), :]

    hc_ref[0:1, :] = lax.fori_loop(0, ng, carry_body, hc_ref[0:1, :])

    def out_body(g, _):
        r0 = pl.multiple_of(g * SUBLANES, SUBLANES)
        rows = pl.ds(r0, SUBLANES)
        cin = cin_ref[pl.ds(g, 1), :]
        for j in range(D_RNN // LANES):
            cj = jnp.broadcast_to(cin[:, j * LANES:(j + 1) * LANES], (SUBLANES, LANES))
            h_ref[j, rows, :] = h_ref[j, rows, :] + a_ref[j, rows, :] * cj
        return 0

    lax.fori_loop(0, ng, out_body, 0)

    for j in range(D_RNN // LANES):
        gr = rnn_ref[:, D_RNN + j * LANES:D_RNN + (j + 1) * LANES].astype(F32)
        y_ref[:, j * LANES:(j + 1) * LANES] = (
            h_ref[j] * jax.nn.gelu(gr, approximate=True)).astype(y_ref.dtype)


def _rnn(rnn, conv_w, conv_b, w_gates, b_a, b_x, lam, bsz, seq):
    n = rnn.shape[0]
    ts = min(TS, seq)
    ns = seq // ts
    ng = ts // SUBLANES
    return pl.pallas_call(
        _rnn_kernel,
        grid=(bsz, ns),
        in_specs=[pl.BlockSpec((ts, W_RNN), lambda b, s: (b * ns + s, 0)),
                  _const_spec((CONV_WIDTH, D_RNN)),
                  _const_spec((1, D_RNN)),
                  _const_spec((D_RNN // GATE_GROUP, GATE_GROUP, 2 * GATE_GROUP)),
                  _const_spec((1, D_RNN)),
                  _const_spec((1, D_RNN)),
                  _const_spec((1, D_RNN))],
        out_specs=pl.BlockSpec((ts, D_RNN), lambda b, s: (b * ns + s, 0)),
        out_shape=jax.ShapeDtypeStruct((n, D_RNN), _MX),
        scratch_shapes=[pltpu.VMEM((ts + SUBLANES, D_RNN), F32),
                        pltpu.VMEM((D_RNN // LANES, ts, LANES), F32),
                        pltpu.VMEM((D_RNN // LANES, ts, LANES), F32),
                        pltpu.VMEM((ng, D_RNN), F32),
                        pltpu.VMEM((ng, D_RNN), F32),
                        pltpu.VMEM((ng, D_RNN), F32),
                        pltpu.VMEM((SUBLANES, D_RNN), F32)],
        compiler_params=pltpu.CompilerParams(
            dimension_semantics=("arbitrary", "arbitrary"), vmem_limit_bytes=VMEM_LIMIT),
        name="rnn",
    )(rnn, conv_w, conv_b, w_gates, b_a, b_x, lam)


def _rope_block(x, cos, sin_lo, sin_hi):
    w = x.shape[1]
    half = ROT_DIM // 2
    return x * cos + pltpu.roll(x, w - half, 1) * sin_lo + pltpu.roll(x, half, 1) * sin_hi


def _prep_kernel(qkv_ref, idx_ref, cos_ref, slo_ref, shi_ref, lng_ref, lnb_ref,
                 q_ref, k_ref, v_ref, iq_ref, ik_ref, iw_ref):
    cos, slo, shi = cos_ref[...], slo_ref[...], shi_ref[...]
    tm = qkv_ref.shape[0]
    tq = q_ref.shape[2]
    scale = (HEAD_DIM ** -0.5) * math.log2(math.e)
    heads_per_blk = LANES // HEAD_DIM

    def put_heads(ref, blk, rb):
        for j in range(heads_per_blk):
            for t in range(tm // tq):
                ref[t, blk * heads_per_blk + j] = rb[t * tq:(t + 1) * tq, j * HEAD_DIM:(j + 1) * HEAD_DIM]

    for blk in range(D_Q // LANES):
        xb = qkv_ref[:, blk * LANES:(blk + 1) * LANES].astype(F32) * scale
        put_heads(q_ref, blk, _rope_block(xb, cos, slo, shi).astype(q_ref.dtype))
    ones_col = (lax.broadcasted_iota(I32, (tm, HEAD_DIM), 1) == 0).astype(v_ref.dtype)
    for blk in range(D_KV // LANES):
        c0 = D_Q + blk * LANES
        rb = _rope_block(qkv_ref[:, c0:c0 + LANES].astype(F32), cos, slo, shi).astype(k_ref.dtype)
        for j in range(heads_per_blk):
            k_ref[blk * heads_per_blk + j] = rb[:, j * HEAD_DIM:(j + 1) * HEAD_DIM]
            v0 = D_Q + D_KV + blk * LANES + j * HEAD_DIM
            v_ref[blk * heads_per_blk + j] = jnp.concatenate(
                [qkv_ref[:, v0:v0 + HEAD_DIM], ones_col], axis=1)
    for blk in range(D_IQ // LANES):
        rb = _rope_block(idx_ref[:, blk * LANES:(blk + 1) * LANES], cos, slo, shi).astype(iq_ref.dtype)
        put_heads(iq_ref, blk, rb)
    tail = idx_ref[:, D_IQ:D_IQ + LANES]
    is_key = lax.broadcasted_iota(I32, (1, LANES), 1) < IDX_DIM
    mu = jnp.sum(jnp.where(is_key, tail, 0.0), axis=-1, keepdims=True) * (1.0 / IDX_DIM)
    cen = jnp.where(is_key, tail - mu, 0.0)
    var = jnp.sum(cen * cen, axis=-1, keepdims=True) * (1.0 / IDX_DIM)
    ik = cen * lax.rsqrt(var + EPS) * lng_ref[...] + lnb_ref[...]
    ik = _rope_block(ik, cos, slo, shi)
    ik_ref[...] = ik[:, :IDX_DIM].astype(ik_ref.dtype)
    iw_ref[...] = tail[:, IDX_DIM:IDX_DIM + N_IDX_HEADS] * ((N_IDX_HEADS ** -0.5) * (IDX_DIM ** -0.5))


def _prep(qkv, idx, cos, slo, shi, ln_g, ln_b, bsz, seq):
    tm = min(TM, seq)
    tq = min(TQ, seq)
    ns = seq // tm
    tok = lambda b, s: (b * ns + s, 0)
    tab = pl.BlockSpec((tm, LANES), lambda b, s: (s, 0))
    tiled = lambda nh: pl.BlockSpec((None, tm // tq, nh, tq, HEAD_DIM), lambda b, s: (b, s, 0, 0, 0))
    return pl.pallas_call(
        _prep_kernel,
        grid=(bsz, ns),
        in_specs=[pl.BlockSpec((tm, W_QKV), tok),
                  pl.BlockSpec((tm, D_IDX_PAD), tok),
                  tab, tab, tab,
                  _const_spec((1, LANES)),
                  _const_spec((1, LANES))],
        out_specs=[tiled(N_HEADS),
                   pl.BlockSpec((None, N_KV_HEADS, tm, HEAD_DIM), lambda b, s: (b, 0, s, 0)),
                   pl.BlockSpec((None, N_KV_HEADS, tm, LANES), lambda b, s: (b, 0, s, 0)),
                   tiled(N_IDX_HEADS),
                   pl.BlockSpec((None, tm, IDX_DIM), lambda b, s: (b, s, 0)),
                   pl.BlockSpec((None, tm, N_IDX_HEADS), lambda b, s: (b, s, 0))],
        out_shape=[jax.ShapeDtypeStruct((bsz, seq // tq, N_HEADS, tq, HEAD_DIM), _MX),
                   jax.ShapeDtypeStruct((bsz, N_KV_HEADS, seq, HEAD_DIM), _MX),
                   jax.ShapeDtypeStruct((bsz, N_KV_HEADS, seq, LANES), _MX),
                   jax.ShapeDtypeStruct((bsz, seq // tq, N_IDX_HEADS, tq, IDX_DIM), _MX),
                   jax.ShapeDtypeStruct((bsz, seq, IDX_DIM), _MX),
                   jax.ShapeDtypeStruct((bsz, seq, N_IDX_HEADS), F32)],
        compiler_params=pltpu.CompilerParams(
            dimension_semantics=("arbitrary", "arbitrary"), vmem_limit_bytes=VMEM_LIMIT),
        name="prep",
    )(qkv, idx, cos, slo, shi, ln_g, ln_b)


_NT = (((1,), (1,)), ((), ()))
REP = N_HEADS // N_KV_HEADS


def _attn_kernel(iq_ref, ik_ref, iw_ref, q_ref, k_ref, v_ref, o_ref,
                 key_ref, bias_ref, m_ref, acc_ref, jmax_ref, *, topk, ck):
    qt = pl.program_id(1)
    tq = o_ref.shape[0]
    seq = ik_ref.shape[0]
    nck = (qt * tq + tq + ck - 1) // ck
    row = qt * tq + lax.broadcasted_iota(I32, (tq, 1), 0)
    lane = lax.broadcasted_iota(I32, (1, ck), 1)
    iw = iw_ref[...]

    iq_all = iq_ref[...].reshape(N_IDX_HEADS * tq, IDX_DIM)

    def score_body(c, _):
        k0 = pl.multiple_of(c * ck, ck)
        logit = lax.dot_general(iq_all, ik_ref[pl.ds(k0, ck), :], _NT, preferred_element_type=F32)
        acc = jnp.zeros((tq, ck), F32)
        for h in range(N_IDX_HEADS):
            acc = acc + jnp.maximum(logit[h * tq:(h + 1) * tq], 0.0) * iw[:, h:h + 1]
        acc = jnp.where(acc == 0.0, 0.0, acc)
        bits = pltpu.bitcast(acc, I32)
        key = bits ^ ((bits >> 31) & 0x7FFFFFFF)
        key_ref[:, pl.ds(k0, ck)] = jnp.where(k0 + lane <= row, key, INT_MIN)
        return 0

    lax.fori_loop(0, nck, score_body, 0)

    def count(pred):
        def body(c, acc):
            k0 = pl.multiple_of(c * ck, ck)
            hit = jnp.where(pred(key_ref[:, pl.ds(k0, ck)], k0), 1.0, 0.0)
            for j in range(ck // LANES):
                acc = acc + hit[:, j * LANES:(j + 1) * LANES]
            return acc
        acc = lax.fori_loop(0, nck, body, jnp.zeros((tq, LANES), F32))
        return jnp.sum(acc, axis=1, keepdims=True)

    kk = jnp.minimum(row + 1, topk).astype(F32)
    n_valid = (row + 1).astype(F32)

    def excess(n_ge):
        return (jnp.max(n_ge - kk) > 0.0).astype(I32)

    def bit_body(carry):
        _, i, thr, n_ge = carry
        cand = thr + lax.shift_left(jnp.int32(1), 31 - i)
        cnt = count(lambda keys, k0: keys >= cand)
        take = cnt >= kk
        thr = jnp.where(take, cand, thr)
        n_ge = jnp.where(take, cnt, n_ge)
        return excess(n_ge) * (i < 31).astype(I32), i + 1, thr, n_ge

    _, _, thr, n_ge = lax.while_loop(
        lambda c: c[0] > 0, bit_body,
        (excess(n_valid), jnp.int32(0), jnp.full((tq, 1), INT_MIN, I32), n_valid))

    jmax_ref[...] = jnp.full((tq, LANES), seq, I32)

    @pl.when(excess(n_ge) > 0)
    def _():
        need = kk - count(lambda keys, k0: keys > thr)
        nbits = max(1, (seq - 1).bit_length())

        def jbit_body(i, x):
            cand = x + lax.shift_left(jnp.int32(1), nbits - 1 - i)
            below = count(lambda keys, k0: (keys == thr) & (k0 + lane < cand))
            return jnp.where(below < need, cand, x)

        x = lax.fori_loop(0, nbits, jbit_body, jnp.zeros((tq, 1), I32))
        jmax_ref[...] = jnp.broadcast_to(x, (tq, LANES))

    jmax = jmax_ref[:, 0:1]

    def bias_body(c, _):
        k0 = pl.multiple_of(c * ck, ck)
        keys = key_ref[:, pl.ds(k0, ck)]
        col = k0 + lane
        sel = ((keys > thr) | ((keys == thr) & (col <= jmax))) & (col <= row)
        bias_ref[:, pl.ds(k0, ck)] = jnp.where(sel, 0.0, NEG_BIG)
        return 0

    lax.fori_loop(0, nck, bias_body, 0)

    m_ref[...] = jnp.full(m_ref.shape, NEG_BIG, F32)
    acc_ref[...] = jnp.zeros(acc_ref.shape, F32)

    def attn_body(c, _):
        k0 = pl.multiple_of(c * ck, ck)
        bias = bias_ref[:, pl.ds(k0, ck)]
        for g in range(N_KV_HEADS):
            q = q_ref[g * REP:(g + 1) * REP].reshape(REP * tq, HEAD_DIM)
            s = lax.dot_general(q, k_ref[g, pl.ds(k0, ck), :], _NT, preferred_element_type=F32)
            s = (s.reshape(REP, tq, ck) + bias[None]).reshape(REP * tq, ck)
            m_old = m_ref[g]
            m_new = jnp.maximum(m_old, jnp.max(s, axis=1, keepdims=True))
            m_ref[g] = m_new
            p = jnp.concatenate(
                [jnp.exp2(s[:, j * LANES:(j + 1) * LANES] - m_new) for j in range(ck // LANES)], axis=1)
            acc_ref[g] = jnp.exp2(m_old - m_new) * acc_ref[g] + jnp.dot(
                p.astype(_MX), v_ref[g, pl.ds(k0, ck), :], preferred_element_type=F32)
        return 0

    lax.fori_loop(0, nck, attn_body, 0)

    for g in range(N_KV_HEADS):
        acc = acc_ref[g]
        out = acc[:, :HEAD_DIM] / acc[:, HEAD_DIM:HEAD_DIM + 1]
        for r in range(REP):
            h = g * REP + r
            o_ref[:, h * HEAD_DIM:(h + 1) * HEAD_DIM] = out[r * tq:(r + 1) * tq].astype(o_ref.dtype)


def _attn(iq, ik, iw, q, k, v, bsz, seq):
    topk = min(TOPK_MAX, seq // 4)
    tq = min(TQ, seq)
    ck = min(CK, seq)
    nq = seq // tq
    tile = lambda nh: pl.BlockSpec((None, None, nh, tq, HEAD_DIM), lambda b, t: (b, t, 0, 0, 0))
    return pl.pallas_call(
        functools.partial(_attn_kernel, topk=topk, ck=ck),
        grid=(bsz, nq),
        in_specs=[tile(N_IDX_HEADS),
                  pl.BlockSpec((None, seq, IDX_DIM), lambda b, t: (b, 0, 0)),
                  pl.BlockSpec((None, tq, N_IDX_HEADS), lambda b, t: (b, t, 0)),
                  tile(N_HEADS),
                  pl.BlockSpec((None, N_KV_HEADS, seq, HEAD_DIM), lambda b, t: (b, 0, 0, 0)),
                  pl.BlockSpec((None, N_KV_HEADS, seq, LANES), lambda b, t: (b, 0, 0, 0))],
        out_specs=pl.BlockSpec((tq, D_Q), lambda b, t: (b * nq + t, 0)),
        out_shape=jax.ShapeDtypeStruct((bsz * seq, D_Q), _MX),
        scratch_shapes=[pltpu.VMEM((tq, seq), I32),
                        pltpu.VMEM((tq, seq), F32),
                        pltpu.VMEM((N_KV_HEADS, REP * tq, LANES), F32),
                        pltpu.VMEM((N_KV_HEADS, REP * tq, LANES), F32),
                        pltpu.VMEM((tq, LANES), I32)],
        compiler_params=pltpu.CompilerParams(
            dimension_semantics=("arbitrary", "arbitrary"), vmem_limit_bytes=VMEM_LIMIT),
        name="attn",
    )(iq, ik, iw, q, k, v)


def _merge_kernel(x_ref, yr_ref, ya_ref, gate_ref, wr_ref, wa_ref, wo_ref, g_ref, o_ref):
    y_a = jnp.dot(yr_ref[...], wr_ref[...], preferred_element_type=F32)
    y_b = jnp.dot(ya_ref[...], wa_ref[...], preferred_element_type=F32)
    g_a = jax.nn.sigmoid(gate_ref[:, :D_MODEL].astype(F32))
    g_b = jax.nn.sigmoid(gate_ref[:, D_MODEL:].astype(F32))
    merged = g_a * y_a + g_b * y_b
    mix = jnp.dot(merged.astype(_MX), wo_ref[...], preferred_element_type=F32)
    o_ref[...] = x_ref[...] + _rms(mix, g_ref[...])


def _merge(x2, y_rnn, y_att, gates, w_rnn_out, w_att_out, w_o, g):
    n = x2.shape[0]
    row = lambda w: pl.BlockSpec((TM, w), lambda i: (i, 0))
    return pl.pallas_call(
        _merge_kernel,
        grid=(n // TM,),
        in_specs=[row(D_MODEL), row(D_RNN), row(D_Q), row(W_GATE),
                  _const_spec((D_RNN, D_MODEL)), _const_spec((D_Q, D_MODEL)),
                  _const_spec((D_MODEL, D_MODEL)), _const_spec((1, D_MODEL))],
        out_specs=row(D_MODEL),
        out_shape=jax.ShapeDtypeStruct((n, D_MODEL), F32),
        compiler_params=pltpu.CompilerParams(
            dimension_semantics=("arbitrary",), vmem_limit_bytes=VMEM_LIMIT),
        name="merge",
    )(x2, y_rnn, y_att, gates, w_rnn_out, w_att_out, w_o, g)


FF_CHUNK = D_FF // 2


def _ffn_kernel(x_ref, gpre_ref, wg_ref, wu_ref, wd_ref, gpost_ref, o_ref):
    x = x_ref[...]
    h = _rms(x, gpre_ref[...]).astype(_MX)
    f = jnp.zeros(x.shape, F32)
    for c in range(0, D_FF, FF_CHUNK):
        gate = jnp.dot(h, wg_ref[:, c:c + FF_CHUNK], preferred_element_type=F32)
        up = jnp.dot(h, wu_ref[:, c:c + FF_CHUNK], preferred_element_type=F32)
        act = (jax.nn.silu(gate) * up).astype(_MX)
        f = f + jnp.dot(act, wd_ref[c:c + FF_CHUNK, :], preferred_element_type=F32)
    o_ref[...] = x + _rms(f, gpost_ref[...])


def _ffn(x1, g_pre, w_gate, w_up, w_down, g_post):
    n = x1.shape[0]
    row = pl.BlockSpec((TM, D_MODEL), lambda i: (i, 0))
    return pl.pallas_call(
        _ffn_kernel,
        grid=(n // TM,),
        in_specs=[row, _const_spec((1, D_MODEL)),
                  _const_spec((D_MODEL, D_FF)), _const_spec((D_MODEL, D_FF)),
                  _const_spec((D_FF, D_MODEL)), _const_spec((1, D_MODEL))],
        out_specs=row,
        out_shape=jax.ShapeDtypeStruct((n, D_MODEL), F32),
        compiler_params=pltpu.CompilerParams(
            dimension_semantics=("arbitrary",), vmem_limit_bytes=VMEM_LIMIT),
        name="ffn",
    )(x1, g_pre, w_gate, w_up, w_down, g_post)


def _rope_tables(seq):
    half = ROT_DIM // 2
    inv_freq = ROPE_THETA ** (-jnp.arange(half, dtype=F32) / half)
    ang = jnp.arange(seq, dtype=F32)[:, None] * inv_freq[None, :]
    cos, sin = jnp.cos(ang), jnp.sin(ang)
    rest = HEAD_DIM - ROT_DIM
    ones, zeros, zh = jnp.ones((seq, rest), F32), jnp.zeros((seq, rest), F32), jnp.zeros((seq, half), F32)
    cos_t = jnp.concatenate([cos, cos, ones], axis=1)
    slo_t = jnp.concatenate([-sin, zh, zeros], axis=1)
    shi_t = jnp.concatenate([zh, sin, zeros], axis=1)
    two = lambda t: jnp.concatenate([t, t], axis=1)
    return two(cos_t), two(slo_t), two(shi_t)


def _gate_weights(w_a, w_x):
    per = GATE_GROUP // RNN_BLOCK

    def diag(w):
        w = w.reshape(D_RNN // GATE_GROUP, per, RNN_BLOCK, RNN_BLOCK)
        eye = jnp.eye(per, dtype=w.dtype)
        return jnp.einsum('gpij,pq->gpiqj', w, eye).reshape(-1, GATE_GROUP, GATE_GROUP)

    return jnp.concatenate([diag(w_a), diag(w_x)], axis=-1).astype(_MX)


def kernel(x, norm_mix_pre, w_in, conv_w, conv_b, rg_w_a, rg_b_a, rg_w_x, rg_b_x, rg_lambda,
           idx_k_ln_g, idx_k_ln_b, w_rnn_out, w_att_out, w_o, norm_mix_post,
           norm_ffn_pre, w_ffn_gate, w_ffn_up, w_ffn_down, norm_ffn_post):
    bsz, seq, _ = x.shape
    depth = w_in.shape[0]
    x2 = x.reshape(bsz * seq, D_MODEL)
    cos, slo, shi = _rope_tables(seq)
    o_idx = W_RNN + W_QKV
    for l in range(depth):
        w = w_in[l]
        w_all = jnp.concatenate(
            [w[:, :o_idx], w[:, o_idx + D_IDX:], w[:, o_idx:o_idx + D_IDX],
             jnp.zeros((D_MODEL, D_IDX_PAD - D_IDX), w.dtype)], axis=1).astype(_MX)
        rnn, qkv, gates, idx = _inproj(x2, norm_mix_pre[l][None], w_all)
        y_rnn = _rnn(rnn, conv_w[l], conv_b[l][None], _gate_weights(rg_w_a[l], rg_w_x[l]),
                     rg_b_a[l][None], rg_b_x[l][None], rg_lambda[l][None], bsz, seq)
        pad_lanes = lambda t: jnp.pad(t, (0, LANES - IDX_DIM))[None]
        q, k, v, iq, ik, iw = _prep(qkv, idx, cos, slo, shi,
                                    pad_lanes(idx_k_ln_g[l]), pad_lanes(idx_k_ln_b[l]), bsz, seq)
        y_att = _attn(iq, ik, iw, q, k, v, bsz, seq)
        x1 = _merge(x2, y_rnn, y_att, gates, w_rnn_out[l].astype(_MX), w_att_out[l].astype(_MX),
                    w_o[l].astype(_MX), norm_mix_post[l][None])
        x2 = _ffn(x1, norm_ffn_pre[l][None], w_ffn_gate[l].astype(_MX), w_ffn_up[l].astype(_MX),
                  w_ffn_down[l].astype(_MX), norm_ffn_post[l][None])
    return x2.reshape(bsz, seq, D_MODEL)
```

```python
import functools
import math

import jax
import jax.numpy as jnp
from jax import lax
from jax.experimental import pallas as pl
from jax.experimental.pallas import tpu as pltpu

F32 = jnp.float32
I32 = jnp.int32
_MX = jnp.bfloat16

D_MODEL = 1024
D_RNN = 1024
N_RNN_BLOCKS = 16
RNN_BLOCK = D_RNN // N_RNN_BLOCKS
CONV_WIDTH = 4
LRU_C = 8.0
N_HEADS = 16
N_KV_HEADS = 4
HEAD_DIM = 64
ROT_DIM = HEAD_DIM // 4
ROPE_THETA = 500000.0
N_IDX_HEADS = 8
IDX_DIM = 64
TOPK_MAX = 256
D_FF = 2816
EPS = 1e-6

D_Q = N_HEADS * HEAD_DIM
D_KV = N_KV_HEADS * HEAD_DIM
D_QKV = D_Q + 2 * D_KV
D_IQ = N_IDX_HEADS * IDX_DIM
D_IDX = D_IQ + IDX_DIM + N_IDX_HEADS
D_IDX_PAD = 640
W_RNN, W_QKV, W_GATE = 2 * D_RNN, D_QKV, 2 * D_MODEL
W_ALL = W_RNN + W_QKV + W_GATE + D_IDX_PAD

LANES = 128
SUBLANES = 8
VMEM_LIMIT = 56 * 1024 * 1024
NEG_BIG = -1e30
INT_MIN = -(2 ** 31)

TM = 512
TS = 512
TQ = 128
CK = 512
GATE_GROUP = 256


def _const_spec(shape):
    nd = len(shape)
    return pl.BlockSpec(shape, lambda *_: (0,) * nd, pipeline_mode=pl.Buffered(1))


def _rms(x, g):
    return x * lax.rsqrt(jnp.mean(x * x, axis=-1, keepdims=True) + EPS) * g


def _inproj_kernel(x_ref, g_ref, w_ref, rnn_ref, qkv_ref, gate_ref, idx_ref):
    h = _rms(x_ref[...], g_ref[...]).astype(_MX)
    col = 0
    for ref, width in ((rnn_ref, W_RNN), (qkv_ref, W_QKV), (gate_ref, W_GATE), (idx_ref, D_IDX_PAD)):
        step = 512 if width % 512 == 0 else width
        for c in range(0, width, step):
            ref[:, c:c + step] = jnp.dot(
                h, w_ref[:, col + c:col + c + step], preferred_element_type=F32).astype(ref.dtype)
        col += width


def _inproj(x2, g, w_all):
    n = x2.shape[0]
    return pl.pallas_call(
        _inproj_kernel,
        grid=(n // TM,),
        in_specs=[pl.BlockSpec((TM, D_MODEL), lambda i: (i, 0)),
                  _const_spec((1, D_MODEL)),
                  _const_spec((D_MODEL, W_ALL))],
        out_specs=[pl.BlockSpec((TM, W_RNN), lambda i: (i, 0)),
                   pl.BlockSpec((TM, W_QKV), lambda i: (i, 0)),
                   pl.BlockSpec((TM, W_GATE), lambda i: (i, 0)),
                   pl.BlockSpec((TM, D_IDX_PAD), lambda i: (i, 0))],
        out_shape=[jax.ShapeDtypeStruct((n, W_RNN), _MX),
                   jax.ShapeDtypeStruct((n, W_QKV), _MX),
                   jax.ShapeDtypeStruct((n, W_GATE), _MX),
                   jax.ShapeDtypeStruct((n, D_IDX_PAD), F32)],
        compiler_params=pltpu.CompilerParams(
            dimension_semantics=("arbitrary",), vmem_limit_bytes=VMEM_LIMIT),
        name="inproj",
    )(x2, g, w_all)


def _rnn_kernel(rnn_ref, cw_ref, cb_ref, wg_ref, ba_ref, bx_ref, lam_ref, y_ref,
                xpad_ref, a_ref, h_ref, sp_ref, sh_ref, cin_ref, hc_ref):
    s = pl.program_id(1)
    ts = y_ref.shape[0]
    ng = ts // SUBLANES

    @pl.when(s == 0)
    def _():
        xpad_ref[0:SUBLANES, :] = jnp.zeros((SUBLANES, D_RNN), F32)
        hc_ref[...] = jnp.zeros_like(hc_ref)

    @pl.when(s != 0)
    def _():
        xpad_ref[0:SUBLANES, :] = xpad_ref[ts:ts + SUBLANES, :]

    xpad_ref[SUBLANES:SUBLANES + ts, :] = rnn_ref[:, 0:D_RNN].astype(F32)

    lam = lam_ref[...]
    nl = -lam
    softplus_nl = jnp.maximum(nl, 0.0) + jnp.log1p(jnp.exp(-jnp.abs(nl)))

    for c in range(0, D_RNN, GATE_GROUP):
        cs = slice(c, c + GATE_GROUP)
        xc = cb_ref[:, cs]
        for k in range(CONV_WIDTH):
            off = SUBLANES - (CONV_WIDTH - 1) + k
            xc = xc + xpad_ref[off:off + ts, cs] * cw_ref[k:k + 1, cs]
        gz = jnp.dot(xc.astype(_MX), wg_ref[c // GATE_GROUP], preferred_element_type=F32)
        r = jax.nn.sigmoid(gz[:, :GATE_GROUP] + ba_ref[:, cs])
        i = jax.nn.sigmoid(gz[:, GATE_GROUP:] + bx_ref[:, cs])
        log_a = (-LRU_C * r) * softplus_nl[:, cs]
        a = jnp.exp(log_a)
        u = jnp.sqrt(-jnp.tanh(log_a) * (a * a + 1.0)) * (i * xc)
        for j in range(GATE_GROUP // LANES):
            a_ref[c // LANES + j] = a[:, j * LANES:(j + 1) * LANES]
            h_ref[c // LANES + j] = u[:, j * LANES:(j + 1) * LANES]

    for j in range(D_RNN // LANES):
        ls = slice(j * LANES, (j + 1) * LANES)
        h = h_ref[j, pl.ds(0, ng, stride=SUBLANES), :]
        p = a_ref[j, pl.ds(0, ng, stride=SUBLANES), :]
        for i in range(1, SUBLANES):
            a_i = a_ref[j, pl.ds(i, ng, stride=SUBLANES), :]
            h = a_i * h + h_ref[j, pl.ds(i, ng, stride=SUBLANES), :]
            p = a_i * p
            h_ref[j, pl.ds(i, ng, stride=SUBLANES), :] = h
            a_ref[j, pl.ds(i, ng, stride=SUBLANES), :] = p
        sp_ref[:, ls] = p
        sh_ref[:, ls] = h

    def carry_body(g, carry):
        cin_ref[pl.ds(g, 1), :] = carry
        return sp_ref[pl.ds(g, 1), :] * carry + sh_ref[pl.ds(g, 1), :]

    hc_ref[0:1, :] = lax.fori_loop(0, ng, carry_body, hc_ref[0:1, :])

    def out_body(g, _):
        r0 = pl.multiple_of(g * SUBLANES, SUBLANES)
        rows = pl.ds(r0, SUBLANES)
        cin = cin_ref[pl.ds(g, 1), :]
        for j in range(D_RNN // LANES):
            cj = jnp.broadcast_to(cin[:, j * LANES:(j + 1) * LANES], (SUBLANES, LANES))
            h_ref[j, rows, :] = h_ref[j, rows, :] + a_ref[j, rows, :] * cj
        return 0

    lax.fori_loop(0, ng, out_body, 0)

    for j in range(D_RNN // LANES):
        gr = rnn_ref[:, D_RNN + j * LANES:D_RNN + (j + 1) * LANES].astype(F32)
        y_ref[:, j * LANES:(j + 1) * LANES] = (
            h_ref[j] * jax.nn.gelu(gr, approximate=True)).astype(y_ref.dtype)


def _rnn(rnn, conv_w, conv_b, w_gates, b_a, b_x, lam, bsz, seq):
    n = rnn.shape[0]
    ts = min(TS, seq)
    ns = seq // ts
    ng = ts // SUBLANES
    return pl.pallas_call(
        _rnn_kernel,
        grid=(bsz, ns),
        in_specs=[pl.BlockSpec((ts, W_RNN), lambda b, s: (b * ns + s, 0)),
                  _const_spec((CONV_WIDTH, D_RNN)),
                  _const_spec((1, D_RNN)),
                  _const_spec((D_RNN // GATE_GROUP, GATE_GROUP, 2 * GATE_GROUP)),
                  _const_spec((1, D_RNN)),
                  _const_spec((1, D_RNN)),
                  _const_spec((1, D_RNN))],
        out_specs=pl.BlockSpec((ts, D_RNN), lambda b, s: (b * ns + s, 0)),
        out_shape=jax.ShapeDtypeStruct((n, D_RNN), _MX),
        scratch_shapes=[pltpu.VMEM((ts + SUBLANES, D_RNN), F32),
                        pltpu.VMEM((D_RNN // LANES, ts, LANES), F32),
                        pltpu.VMEM((D_RNN // LANES, ts, LANES), F32),
                        pltpu.VMEM((ng, D_RNN), F32),
                        pltpu.VMEM((ng, D_RNN), F32),
                        pltpu.VMEM((ng, D_RNN), F32),
                        pltpu.VMEM((SUBLANES, D_RNN), F32)],
        compiler_params=pltpu.CompilerParams(
            dimension_semantics=("arbitrary", "arbitrary"), vmem_limit_bytes=VMEM_LIMIT),
        name="rnn",
    )(rnn, conv_w, conv_b, w_gates, b_a, b_x, lam)


def _rope_block(x, cos, sin_lo, sin_hi):
    w = x.shape[1]
    half = ROT_DIM // 2
    return x * cos + pltpu.roll(x, w - half, 1) * sin_lo + pltpu.roll(x, half, 1) * sin_hi


def _prep_kernel(qkv_ref, idx_ref, cos_ref, slo_ref, shi_ref, lng_ref, lnb_ref,
                 q_ref, k_ref, vt_ref, iq_ref, ik_ref, iwt_ref):
    cos, slo, shi = cos_ref[...], slo_ref[...], shi_ref[...]
    tm = qkv_ref.shape[0]
    tq = q_ref.shape[2]
    scale = (HEAD_DIM ** -0.5) * math.log2(math.e)
    heads_per_blk = LANES // HEAD_DIM

    def put_heads(ref, blk, rb):
        for j in range(heads_per_blk):
            for t in range(tm // tq):
                ref[t, blk * heads_per_blk + j] = rb[t * tq:(t + 1) * tq, j * HEAD_DIM:(j + 1) * HEAD_DIM]

    for blk in range(D_Q // LANES):
        xb = qkv_ref[:, blk * LANES:(blk + 1) * LANES].astype(F32) * scale
        put_heads(q_ref, blk, _rope_block(xb, cos, slo, shi).astype(q_ref.dtype))
    ones_col = (lax.broadcasted_iota(I32, (tm, HEAD_DIM), 1) == 0).astype(F32)
    for blk in range(D_KV // LANES):
        c0 = D_Q + blk * LANES
        rb = _rope_block(qkv_ref[:, c0:c0 + LANES].astype(F32), cos, slo, shi).astype(k_ref.dtype)
        for j in range(heads_per_blk):
            k_ref[blk * heads_per_blk + j] = rb[:, j * HEAD_DIM:(j + 1) * HEAD_DIM]
            v0 = D_Q + D_KV + blk * LANES + j * HEAD_DIM
            v_aug = jnp.concatenate([qkv_ref[:, v0:v0 + HEAD_DIM].astype(F32), ones_col], axis=1)
            vt_ref[blk * heads_per_blk + j] = v_aug.T.astype(vt_ref.dtype)
    for blk in range(D_IQ // LANES):
        rb = _rope_block(idx_ref[:, blk * LANES:(blk + 1) * LANES], cos, slo, shi).astype(iq_ref.dtype)
        put_heads(iq_ref, blk, rb)
    tail = idx_ref[:, D_IQ:D_IQ + LANES]
    is_key = lax.broadcasted_iota(I32, (1, LANES), 1) < IDX_DIM
    mu = jnp.sum(jnp.where(is_key, tail, 0.0), axis=-1, keepdims=True) * (1.0 / IDX_DIM)
    cen = jnp.where(is_key, tail - mu, 0.0)
    var = jnp.sum(cen * cen, axis=-1, keepdims=True) * (1.0 / IDX_DIM)
    ik = cen * lax.rsqrt(var + EPS) * lng_ref[...] + lnb_ref[...]
    ik = _rope_block(ik, cos, slo, shi)
    ik_ref[...] = ik[:, :IDX_DIM].astype(ik_ref.dtype)
    iwt_ref[...] = tail.T[IDX_DIM:IDX_DIM + N_IDX_HEADS, :] * ((N_IDX_HEADS ** -0.5) * (IDX_DIM ** -0.5))


def _prep(qkv, idx, cos, slo, shi, ln_g, ln_b, bsz, seq):
    tm = min(TM, seq)
    tq = min(TQ, seq)
    ns = seq // tm
    tok = lambda b, s: (b * ns + s, 0)
    tab = pl.BlockSpec((tm, LANES), lambda b, s: (s, 0))
    tiled = lambda nh: pl.BlockSpec((None, tm // tq, nh, tq, HEAD_DIM), lambda b, s: (b, s, 0, 0, 0))
    return pl.pallas_call(
        _prep_kernel,
        grid=(bsz, ns),
        in_specs=[pl.BlockSpec((tm, W_QKV), tok),
                  pl.BlockSpec((tm, D_IDX_PAD), tok),
                  tab, tab, tab,
                  _const_spec((1, LANES)),
                  _const_spec((1, LANES))],
        out_specs=[tiled(N_HEADS),
                   pl.BlockSpec((None, N_KV_HEADS, tm, HEAD_DIM), lambda b, s: (b, 0, s, 0)),
                   pl.BlockSpec((None, N_KV_HEADS, LANES, tm), lambda b, s: (b, 0, 0, s)),
                   tiled(N_IDX_HEADS),
                   pl.BlockSpec((None, tm, IDX_DIM), lambda b, s: (b, s, 0)),
                   pl.BlockSpec((None, N_IDX_HEADS, tm), lambda b, s: (b, 0, s))],
        out_shape=[jax.ShapeDtypeStruct((bsz, seq // tq, N_HEADS, tq, HEAD_DIM), _MX),
                   jax.ShapeDtypeStruct((bsz, N_KV_HEADS, seq, HEAD_DIM), _MX),
                   jax.ShapeDtypeStruct((bsz, N_KV_HEADS, LANES, seq), _MX),
                   jax.ShapeDtypeStruct((bsz, seq // tq, N_IDX_HEADS, tq, IDX_DIM), _MX),
                   jax.ShapeDtypeStruct((bsz, seq, IDX_DIM), _MX),
                   jax.ShapeDtypeStruct((bsz, N_IDX_HEADS, seq), F32)],
        compiler_params=pltpu.CompilerParams(
            dimension_semantics=("arbitrary", "arbitrary"), vmem_limit_bytes=VMEM_LIMIT),
        name="prep",
    )(qkv, idx, cos, slo, shi, ln_g, ln_b)


_NT = (((1,), (1,)), ((), ()))
REP = N_HEADS // N_KV_HEADS
KEY_BITS = 32
COUNT_ROWS = 64


def _attn_kernel(iq_ref, ik_ref, iwt_ref, q_ref, k_ref, vt_ref, o_ref,
                 key_ref, bias_ref, m_ref, acc_ref, jmax_ref, *, topk, ck):
    qt = pl.program_id(1)
    tq = o_ref.shape[0]
    seq = ik_ref.shape[0]
    nck = (qt * tq + tq + ck - 1) // ck
    qpos = qt * tq + lax.broadcasted_iota(I32, (1, tq), 1)
    krow = lax.broadcasted_iota(I32, (ck, 1), 0)
    iwt = iwt_ref[...]

    def fold_rows(x, op):
        y = op(x.reshape(ck // COUNT_ROWS, COUNT_ROWS, x.shape[1]), axis=0)
        y = op(y.reshape(COUNT_ROWS // SUBLANES, SUBLANES, x.shape[1]), axis=0)
        return op(y, axis=0, keepdims=True)

    iq_all = iq_ref[...].reshape(N_IDX_HEADS * tq, IDX_DIM)

    def score_body(c, _):
        k0 = pl.multiple_of(c * ck, ck)
        logit = lax.dot_general(ik_ref[pl.ds(k0, ck), :], iq_all, _NT, preferred_element_type=F32)
        acc = jnp.zeros((ck, tq), F32)
        for h in range(N_IDX_HEADS):
            acc = acc + jnp.maximum(logit[:, h * tq:(h + 1) * tq], 0.0) * iwt[h:h + 1, :]
        acc = jnp.where(acc == 0.0, 0.0, acc)
        bits = pltpu.bitcast(acc, I32)
        key = bits ^ ((bits >> 31) & 0x7FFFFFFF)
        key_ref[pl.ds(k0, ck), :] = jnp.where(k0 + krow <= qpos, key, INT_MIN)
        return 0

    lax.fori_loop(0, nck, score_body, 0)

    def count(pred):
        def body(c, acc):
            k0 = pl.multiple_of(c * ck, ck)
            hit = jnp.where(pred(key_ref[pl.ds(k0, ck), :], k0), 1.0, 0.0)
            return acc + jnp.sum(hit.reshape(ck // COUNT_ROWS, COUNT_ROWS, tq), axis=0)
        acc = lax.fori_loop(0, nck, body, jnp.zeros((COUNT_ROWS, tq), F32))
        return jnp.sum(acc, axis=0, keepdims=True)

    kk = jnp.minimum(qpos + 1, topk).astype(F32)

    def bit_body(i, carry):
        thr, n_ge = carry
        cand = thr + lax.shift_left(jnp.int32(1), KEY_BITS - 1 - i)
        cnt = count(lambda keys, k0: keys >= cand)
        take = cnt >= kk
        return jnp.where(take, cand, thr), jnp.where(take, cnt, n_ge)

    thr, n_ge = lax.fori_loop(
        0, KEY_BITS, bit_body, (jnp.full((1, tq), INT_MIN, I32), (qpos + 1).astype(F32)))

    jmax_ref[...] = jnp.full(jmax_ref.shape, seq, I32)

    @pl.when(jnp.max(n_ge - kk) > 0.0)
    def _():
        need = kk - count(lambda keys, k0: keys > thr)
        nbits = max(1, (seq - 1).bit_length())

        def jbit_body(i, x):
            cand = x + lax.shift_left(jnp.int32(1), nbits - 1 - i)
            below = count(lambda keys, k0: (keys == thr) & (k0 + krow < cand))
            return jnp.where(below < need, cand, x)

        x = lax.fori_loop(0, nbits, jbit_body, jnp.zeros((1, tq), I32))
        jmax_ref[...] = jnp.broadcast_to(x, jmax_ref.shape)

    jmax = jmax_ref[0:1, :]

    def bias_body(c, _):
        k0 = pl.multiple_of(c * ck, ck)
        keys = key_ref[pl.ds(k0, ck), :]
        kpos = k0 + krow
        sel = ((keys > thr) | ((keys == thr) & (kpos <= jmax))) & (kpos <= qpos)
        bias_ref[pl.ds(k0, ck), :] = jnp.where(sel, 0.0, NEG_BIG)
        return 0

    lax.fori_loop(0, nck, bias_body, 0)

    m_ref[...] = jnp.full(m_ref.shape, NEG_BIG, F32)
    acc_ref[...] = jnp.zeros(acc_ref.shape, F32)

    def attn_body(c, _):
        k0 = pl.multiple_of(c * ck, ck)
        bias = bias_ref[pl.ds(k0, ck), :]

        def logits(g):
            q = q_ref[g * REP:(g + 1) * REP].reshape(REP * tq, HEAD_DIM)
            return lax.dot_general(k_ref[g, pl.ds(k0, ck), :], q, _NT, preferred_element_type=F32)

        s_next = logits(0)
        for g in range(N_KV_HEADS):
            s = s_next
            if g + 1 < N_KV_HEADS:
                s_next = logits(g + 1)
            s = jnp.concatenate([s[:, r * tq:(r + 1) * tq] + bias for r in range(REP)], axis=1)
            m_old = m_ref[g]
            m_new = jnp.maximum(m_old, fold_rows(s, jnp.max))
            m_ref[g] = m_new
            p = jnp.exp2(s - m_new).astype(_MX)
            acc_ref[g] = jnp.exp2(m_old - m_new) * acc_ref[g] + jnp.dot(
                vt_ref[g, :, pl.ds(k0, ck)], p, preferred_element_type=F32)
        return 0

    lax.fori_loop(0, nck, attn_body, 0)

    for g in range(N_KV_HEADS):
        acc = acc_ref[g]
        out = acc[:HEAD_DIM, :] / acc[HEAD_DIM:HEAD_DIM + 1, :]
        for r in range(0, REP, 2):
            pair = jnp.concatenate([out[:, r * tq:(r + 1) * tq], out[:, (r + 1) * tq:(r + 2) * tq]], axis=0)
            h = g * REP + r
            o_ref[:, h * HEAD_DIM:(h + 2) * HEAD_DIM] = pair.T.astype(o_ref.dtype)


def _attn(iq, ik, iwt, q, k, vt, bsz, seq):
    topk = min(TOPK_MAX, seq // 4)
    tq = min(TQ, seq)
    ck = min(CK, seq)
    nq = seq // tq
    tile = lambda nh: pl.BlockSpec((None, None, nh, tq, HEAD_DIM), lambda b, t: (b, t, 0, 0, 0))
    return pl.pallas_call(
        functools.partial(_attn_kernel, topk=topk, ck=ck),
        grid=(bsz, nq),
        in_specs=[tile(N_IDX_HEADS),
                  pl.BlockSpec((None, seq, IDX_DIM), lambda b, t: (b, 0, 0)),
                  pl.BlockSpec((None, N_IDX_HEADS, tq), lambda b, t: (b, 0, t)),
                  tile(N_HEADS),
                  pl.BlockSpec((None, N_KV_HEADS, seq, HEAD_DIM), lambda b, t: (b, 0, 0, 0)),
                  pl.BlockSpec((None, N_KV_HEADS, LANES, seq), lambda b, t: (b, 0, 0, 0))],
        out_specs=pl.BlockSpec((tq, D_Q), lambda b, t: (b * nq + t, 0)),
        out_shape=jax.ShapeDtypeStruct((bsz * seq, D_Q), _MX),
        scratch_shapes=[pltpu.VMEM((seq, tq), I32),
                        pltpu.VMEM((seq, tq), F32),
                        pltpu.VMEM((N_KV_HEADS, 1, REP * tq), F32),
                        pltpu.VMEM((N_KV_HEADS, LANES, REP * tq), F32),
                        pltpu.VMEM((SUBLANES, tq), I32)],
        compiler_params=pltpu.CompilerParams(
            dimension_semantics=("arbitrary", "arbitrary"), vmem_limit_bytes=VMEM_LIMIT),
        name="attn",
    )(iq, ik, iwt, q, k, vt)


def _merge_kernel(x_ref, yr_ref, ya_ref, gate_ref, wr_ref, wa_ref, wo_ref, g_ref, o_ref):
    y_a = jnp.dot(yr_ref[...], wr_ref[...], preferred_element_type=F32)
    y_b = jnp.dot(ya_ref[...], wa_ref[...], preferred_element_type=F32)
    g_a = jax.nn.sigmoid(gate_ref[:, :D_MODEL].astype(F32))
    g_b = jax.nn.sigmoid(gate_ref[:, D_MODEL:].astype(F32))
    merged = g_a * y_a + g_b * y_b
    mix = jnp.dot(merged.astype(_MX), wo_ref[...], preferred_element_type=F32)
    o_ref[...] = x_ref[...] + _rms(mix, g_ref[...])


def _merge(x2, y_rnn, y_att, gates, w_rnn_out, w_att_out, w_o, g):
    n = x2.shape[0]
    row = lambda w: pl.BlockSpec((TM, w), lambda i: (i, 0))
    return pl.pallas_call(
        _merge_kernel,
        grid=(n // TM,),
        in_specs=[row(D_MODEL), row(D_RNN), row(D_Q), row(W_GATE),
                  _const_spec((D_RNN, D_MODEL)), _const_spec((D_Q, D_MODEL)),
                  _const_spec((D_MODEL, D_MODEL)), _const_spec((1, D_MODEL))],
        out_specs=row(D_MODEL),
        out_shape=jax.ShapeDtypeStruct((n, D_MODEL), F32),
        compiler_params=pltpu.CompilerParams(
            dimension_semantics=("arbitrary",), vmem_limit_bytes=VMEM_LIMIT),
        name="merge",
    )(x2, y_rnn, y_att, gates, w_rnn_out, w_att_out, w_o, g)


FF_CHUNK = D_FF // 2


def _ffn_kernel(x_ref, gpre_ref, wg_ref, wu_ref, wd_ref, gpost_ref, o_ref):
    x = x_ref[...]
    h = _rms(x, gpre_ref[...]).astype(_MX)
    f = jnp.zeros(x.shape, F32)
    for c in range(0, D_FF, FF_CHUNK):
        gate = jnp.dot(h, wg_ref[:, c:c + FF_CHUNK], preferred_element_type=F32)
        up = jnp.dot(h, wu_ref[:, c:c + FF_CHUNK], preferred_element_type=F32)
        act = (jax.nn.silu(gate) * up).astype(_MX)
        f = f + jnp.dot(act, wd_ref[c:c + FF_CHUNK, :], preferred_element_type=F32)
    o_ref[...] = x + _rms(f, gpost_ref[...])


def _ffn(x1, g_pre, w_gate, w_up, w_down, g_post):
    n = x1.shape[0]
    row = pl.BlockSpec((TM, D_MODEL), lambda i: (i, 0))
    return pl.pallas_call(
        _ffn_kernel,
        grid=(n // TM,),
        in_specs=[row, _const_spec((1, D_MODEL)),
                  _const_spec((D_MODEL, D_FF)), _const_spec((D_MODEL, D_FF)),
                  _const_spec((D_FF, D_MODEL)), _const_spec((1, D_MODEL))],
        out_specs=row,
        out_shape=jax.ShapeDtypeStruct((n, D_MODEL), F32),
        compiler_params=pltpu.CompilerParams(
            dimension_semantics=("arbitrary",), vmem_limit_bytes=VMEM_LIMIT),
        name="ffn",
    )(x1, g_pre, w_gate, w_up, w_down, g_post)


def _rope_tables(seq):
    half = ROT_DIM // 2
    inv_freq = ROPE_THETA ** (-jnp.arange(half, dtype=F32) / half)
    ang = jnp.arange(seq, dtype=F32)[:, None] * inv_freq[None, :]
    cos, sin = jnp.cos(ang), jnp.sin(ang)
    rest = HEAD_DIM - ROT_DIM
    ones, zeros, zh = jnp.ones((seq, rest), F32), jnp.zeros((seq, rest), F32), jnp.zeros((seq, half), F32)
    cos_t = jnp.concatenate([cos, cos, ones], axis=1)
    slo_t = jnp.concatenate([-sin, zh, zeros], axis=1)
    shi_t = jnp.concatenate([zh, sin, zeros], axis=1)
    two = lambda t: jnp.concatenate([t, t], axis=1)
    return two(cos_t), two(slo_t), two(shi_t)


def _gate_weights(w_a, w_x):
    per = GATE_GROUP // RNN_BLOCK

    def diag(w):
        w = w.reshape(D_RNN // GATE_GROUP, per, RNN_BLOCK, RNN_BLOCK)
        eye = jnp.eye(per, dtype=w.dtype)
        return jnp.einsum('gpij,pq->gpiqj', w, eye).reshape(-1, GATE_GROUP, GATE_GROUP)

    return jnp.concatenate([diag(w_a), diag(w_x)], axis=-1).astype(_MX)


def kernel(x, norm_mix_pre, w_in, conv_w, conv_b, rg_w_a, rg_b_a, rg_w_x, rg_b_x, rg_lambda,
           idx_k_ln_g, idx_k_ln_b, w_rnn_out, w_att_out, w_o, norm_mix_post,
           norm_ffn_pre, w_ffn_gate, w_ffn_up, w_ffn_down, norm_ffn_post):
    bsz, seq, _ = x.shape
    depth = w_in.shape[0]
    x2 = x.reshape(bsz * seq, D_MODEL)
    cos, slo, shi = _rope_tables(seq)
    o_idx = W_RNN + W_QKV
    for l in range(depth):
        w = w_in[l]
        w_all = jnp.concatenate(
            [w[:, :o_idx], w[:, o_idx + D_IDX:], w[:, o_idx:o_idx + D_IDX],
             jnp.zeros((D_MODEL, D_IDX_PAD - D_IDX), w.dtype)], axis=1).astype(_MX)
        rnn, qkv, gates, idx = _inproj(x2, norm_mix_pre[l][None], w_all)
        y_rnn = _rnn(rnn, conv_w[l], conv_b[l][None], _gate_weights(rg_w_a[l], rg_w_x[l]),
                     rg_b_a[l][None], rg_b_x[l][None], rg_lambda[l][None], bsz, seq)
        pad_lanes = lambda t: jnp.pad(t, (0, LANES - IDX_DIM))[None]
        q, k, vt, iq, ik, iwt = _prep(qkv, idx, cos, slo, shi,
                                      pad_lanes(idx_k_ln_g[l]), pad_lanes(idx_k_ln_b[l]), bsz, seq)
        y_att = _attn(iq, ik, iwt, q, k, vt, bsz, seq)
        x1 = _merge(x2, y_rnn, y_att, gates, w_rnn_out[l].astype(_MX), w_att_out[l].astype(_MX),
                    w_o[l].astype(_MX), norm_mix_post[l][None])
        x2 = _ffn(x1, norm_ffn_pre[l][None], w_ffn_gate[l].astype(_MX), w_ffn_up[l].astype(_MX),
                  w_ffn_down[l].astype(_MX), norm_ffn_post[l][None])
    return x2.reshape(bsz, seq, D_MODEL)
```

```python
import functools
import math

import jax
import jax.numpy as jnp
from jax import lax
from jax.experimental import pallas as pl
from jax.experimental.pallas import tpu as pltpu

F32 = jnp.float32
I32 = jnp.int32
_MX = jnp.bfloat16

D_MODEL = 1024
D_RNN = 1024
N_RNN_BLOCKS = 16
RNN_BLOCK = D_RNN // N_RNN_BLOCKS
CONV_WIDTH = 4
LRU_C = 8.0
N_HEADS = 16
N_KV_HEADS = 4
HEAD_DIM = 64
ROT_DIM = HEAD_DIM // 4
ROPE_THETA = 500000.0
N_IDX_HEADS = 8
IDX_DIM = 64
TOPK_MAX = 256
D_FF = 2816
EPS = 1e-6

D_Q = N_HEADS * HEAD_DIM
D_KV = N_KV_HEADS * HEAD_DIM
D_QKV = D_Q + 2 * D_KV
D_IQ = N_IDX_HEADS * IDX_DIM
D_IDX = D_IQ + IDX_DIM + N_IDX_HEADS
D_IDX_PAD = 640
W_RNN, W_QKV, W_GATE = 2 * D_RNN, D_QKV, 2 * D_MODEL
W_ALL = W_RNN + W_QKV + W_GATE + D_IDX_PAD

LANES = 128
SUBLANES = 8
VMEM_LIMIT = 56 * 1024 * 1024
NEG_BIG = -1e30
INT_MIN = -(2 ** 31)

TM = 512
TS = 512
TQ = 256
CK = 512
GATE_GROUP = 256


def _const_spec(shape):
    nd = len(shape)
    return pl.BlockSpec(shape, lambda *_: (0,) * nd, pipeline_mode=pl.Buffered(1))


def _rms(x, g):
    return x * lax.rsqrt(jnp.mean(x * x, axis=-1, keepdims=True) + EPS) * g


def _inproj_kernel(x_ref, g_ref, w_ref, rnn_ref, qkv_ref, gate_ref, idx_ref):
    h = _rms(x_ref[...], g_ref[...]).astype(_MX)
    col = 0
    for ref, width in ((rnn_ref, W_RNN), (qkv_ref, W_QKV), (gate_ref, W_GATE), (idx_ref, D_IDX_PAD)):
        step = 512 if width % 512 == 0 else width
        for c in range(0, width, step):
            ref[:, c:c + step] = jnp.dot(
                h, w_ref[:, col + c:col + c + step], preferred_element_type=F32).astype(ref.dtype)
        col += width


def _inproj(x2, g, w_all):
    n = x2.shape[0]
    return pl.pallas_call(
        _inproj_kernel,
        grid=(n // TM,),
        in_specs=[pl.BlockSpec((TM, D_MODEL), lambda i: (i, 0)),
                  _const_spec((1, D_MODEL)),
                  _const_spec((D_MODEL, W_ALL))],
        out_specs=[pl.BlockSpec((TM, W_RNN), lambda i: (i, 0)),
                   pl.BlockSpec((TM, W_QKV), lambda i: (i, 0)),
                   pl.BlockSpec((TM, W_GATE), lambda i: (i, 0)),
                   pl.BlockSpec((TM, D_IDX_PAD), lambda i: (i, 0))],
        out_shape=[jax.ShapeDtypeStruct((n, W_RNN), _MX),
                   jax.ShapeDtypeStruct((n, W_QKV), _MX),
                   jax.ShapeDtypeStruct((n, W_GATE), _MX),
                   jax.ShapeDtypeStruct((n, D_IDX_PAD), F32)],
        compiler_params=pltpu.CompilerParams(
            dimension_semantics=("arbitrary",), vmem_limit_bytes=VMEM_LIMIT),
        name="inproj",
    )(x2, g, w_all)


def _rnn_kernel(rnn_ref, cw_ref, cb_ref, wg_ref, ba_ref, bx_ref, lam_ref, y_ref,
                xpad_ref, a_ref, h_ref, sp_ref, sh_ref, cin_ref, hc_ref):
    s = pl.program_id(1)
    ts = y_ref.shape[0]
    ng = ts // SUBLANES

    @pl.when(s == 0)
    def _():
        xpad_ref[0:SUBLANES, :] = jnp.zeros((SUBLANES, D_RNN), F32)
        hc_ref[...] = jnp.zeros_like(hc_ref)

    @pl.when(s != 0)
    def _():
        xpad_ref[0:SUBLANES, :] = xpad_ref[ts:ts + SUBLANES, :]

    xpad_ref[SUBLANES:SUBLANES + ts, :] = rnn_ref[:, 0:D_RNN].astype(F32)

    lam = lam_ref[...]
    nl = -lam
    softplus_nl = jnp.maximum(nl, 0.0) + jnp.log1p(jnp.exp(-jnp.abs(nl)))

    for c in range(0, D_RNN, GATE_GROUP):
        cs = slice(c, c + GATE_GROUP)
        xc = cb_ref[:, cs]
        for k in range(CONV_WIDTH):
            off = SUBLANES - (CONV_WIDTH - 1) + k
            xc = xc + xpad_ref[off:off + ts, cs] * cw_ref[k:k + 1, cs]
        gz = jnp.dot(xc.astype(_MX), wg_ref[c // GATE_GROUP], preferred_element_type=F32)
        r = jax.nn.sigmoid(gz[:, :GATE_GROUP] + ba_ref[:, cs])
        i = jax.nn.sigmoid(gz[:, GATE_GROUP:] + bx_ref[:, cs])
        log_a = (-LRU_C * r) * softplus_nl[:, cs]
        a = jnp.exp(log_a)
        u = jnp.sqrt(-jnp.tanh(log_a) * (a * a + 1.0)) * (i * xc)
        for j in range(GATE_GROUP // LANES):
            a_ref[c // LANES + j] = a[:, j * LANES:(j + 1) * LANES]
            h_ref[c // LANES + j] = u[:, j * LANES:(j + 1) * LANES]

    for j in range(D_RNN // LANES):
        ls = slice(j * LANES, (j + 1) * LANES)
        h = h_ref[j, pl.ds(0, ng, stride=SUBLANES), :]
        p = a_ref[j, pl.ds(0, ng, stride=SUBLANES), :]
        for i in range(1, SUBLANES):
            a_i = a_ref[j, pl.ds(i, ng, stride=SUBLANES), :]
            h = a_i * h + h_ref[j, pl.ds(i, ng, stride=SUBLANES), :]
            p = a_i * p
            h_ref[j, pl.ds(i, ng, stride=SUBLANES), :] = h
            a_ref[j, pl.ds(i, ng, stride=SUBLANES), :] = p
        sp_ref[:, ls] = p
        sh_ref[:, ls] = h

    def carry_body(g, carry):
        cin_ref[pl.ds(g, 1), :] = carry
        return sp_ref[pl.ds(g, 1), :] * carry + sh_ref[pl.ds(g, 1), :]

    hc_ref[0:1, :] = lax.fori_loop(0, ng, carry_body, hc_ref[0:1, :])

    def out_body(g, _):
        r0 = pl.multiple_of(g * SUBLANES, SUBLANES)
        rows = pl.ds(r0, SUBLANES)
        cin = cin_ref[pl.ds(g, 1), :]
        for j in range(D_RNN // LANES):
            cj = jnp.broadcast_to(cin[:, j * LANES:(j + 1) * LANES], (SUBLANES, LANES))
            h_ref[j, rows, :] = h_ref[j, rows, :] + a_ref[j, rows, :] * cj
        return 0

    lax.fori_loop(0, ng, out_body, 0)

    for j in range(D_RNN // LANES):
        gr = rnn_ref[:, D_RNN + j * LANES:D_RNN + (j + 1) * LANES].astype(F32)
        y_ref[:, j * LANES:(j + 1) * LANES] = (
            h_ref[j] * jax.nn.gelu(gr, approximate=True)).astype(y_ref.dtype)


def _rnn(rnn, conv_w, conv_b, w_gates, b_a, b_x, lam, bsz, seq):
    n = rnn.shape[0]
    ts = min(TS, seq)
    ns = seq // ts
    ng = ts // SUBLANES
    return pl.pallas_call(
        _rnn_kernel,
        grid=(bsz, ns),
        in_specs=[pl.BlockSpec((ts, W_RNN), lambda b, s: (b * ns + s, 0)),
                  _const_spec((CONV_WIDTH, D_RNN)),
                  _const_spec((1, D_RNN)),
                  _const_spec((D_RNN // GATE_GROUP, GATE_GROUP, 2 * GATE_GROUP)),
                  _const_spec((1, D_RNN)),
                  _const_spec((1, D_RNN)),
                  _const_spec((1, D_RNN))],
        out_specs=pl.BlockSpec((ts, D_RNN), lambda b, s: (b * ns + s, 0)),
        out_shape=jax.ShapeDtypeStruct((n, D_RNN), _MX),
        scratch_shapes=[pltpu.VMEM((ts + SUBLANES, D_RNN), F32),
                        pltpu.VMEM((D_RNN // LANES, ts, LANES), F32),
                        pltpu.VMEM((D_RNN // LANES, ts, LANES), F32),
                        pltpu.VMEM((ng, D_RNN), F32),
                        pltpu.VMEM((ng, D_RNN), F32),
                        pltpu.VMEM((ng, D_RNN), F32),
                        pltpu.VMEM((SUBLANES, D_RNN), F32)],
        compiler_params=pltpu.CompilerParams(
            dimension_semantics=("arbitrary", "arbitrary"), vmem_limit_bytes=VMEM_LIMIT),
        name="rnn",
    )(rnn, conv_w, conv_b, w_gates, b_a, b_x, lam)


def _rope_block(x, cos, sin_lo, sin_hi):
    w = x.shape[1]
    half = ROT_DIM // 2
    return x * cos + pltpu.roll(x, w - half, 1) * sin_lo + pltpu.roll(x, half, 1) * sin_hi


def _prep_kernel(qkv_ref, idx_ref, cos_ref, slo_ref, shi_ref, lng_ref, lnb_ref,
                 q_ref, k_ref, vt_ref, iq_ref, ik_ref, iwt_ref):
    cos, slo, shi = cos_ref[...], slo_ref[...], shi_ref[...]
    tm = qkv_ref.shape[0]
    tq = q_ref.shape[2]
    scale = (HEAD_DIM ** -0.5) * math.log2(math.e)
    heads_per_blk = LANES // HEAD_DIM

    def put_heads(ref, blk, rb):
        for j in range(heads_per_blk):
            for t in range(tm // tq):
                ref[t, blk * heads_per_blk + j] = rb[t * tq:(t + 1) * tq, j * HEAD_DIM:(j + 1) * HEAD_DIM]

    for blk in range(D_Q // LANES):
        xb = qkv_ref[:, blk * LANES:(blk + 1) * LANES].astype(F32) * scale
        put_heads(q_ref, blk, _rope_block(xb, cos, slo, shi).astype(q_ref.dtype))
    ones_col = (lax.broadcasted_iota(I32, (tm, HEAD_DIM), 1) == 0).astype(F32)
    for blk in range(D_KV // LANES):
        c0 = D_Q + blk * LANES
        rb = _rope_block(qkv_ref[:, c0:c0 + LANES].astype(F32), cos, slo, shi).astype(k_ref.dtype)
        for j in range(heads_per_blk):
            k_ref[blk * heads_per_blk + j] = rb[:, j * HEAD_DIM:(j + 1) * HEAD_DIM]
            v0 = D_Q + D_KV + blk * LANES + j * HEAD_DIM
            v_aug = jnp.concatenate([qkv_ref[:, v0:v0 + HEAD_DIM].astype(F32), ones_col], axis=1)
            vt_ref[blk * heads_per_blk + j] = v_aug.T.astype(vt_ref.dtype)
    for blk in range(D_IQ // LANES):
        rb = _rope_block(idx_ref[:, blk * LANES:(blk + 1) * LANES], cos, slo, shi).astype(iq_ref.dtype)
        put_heads(iq_ref, blk, rb)
    tail = idx_ref[:, D_IQ:D_IQ + LANES]
    is_key = lax.broadcasted_iota(I32, (1, LANES), 1) < IDX_DIM
    mu = jnp.sum(jnp.where(is_key, tail, 0.0), axis=-1, keepdims=True) * (1.0 / IDX_DIM)
    cen = jnp.where(is_key, tail - mu, 0.0)
    var = jnp.sum(cen * cen, axis=-1, keepdims=True) * (1.0 / IDX_DIM)
    ik = cen * lax.rsqrt(var + EPS) * lng_ref[...] + lnb_ref[...]
    ik = _rope_block(ik, cos, slo, shi)
    ik_ref[...] = ik[:, :IDX_DIM].astype(ik_ref.dtype)
    iwt_ref[...] = tail.T[IDX_DIM:IDX_DIM + N_IDX_HEADS, :] * ((N_IDX_HEADS ** -0.5) * (IDX_DIM ** -0.5))


def _prep(qkv, idx, cos, slo, shi, ln_g, ln_b, bsz, seq):
    tm = min(TM, seq)
    tq = min(TQ, seq)
    ns = seq // tm
    tok = lambda b, s: (b * ns + s, 0)
    tab = pl.BlockSpec((tm, LANES), lambda b, s: (s, 0))
    tiled = lambda nh: pl.BlockSpec((None, tm // tq, nh, tq, HEAD_DIM), lambda b, s: (b, s, 0, 0, 0))
    return pl.pallas_call(
        _prep_kernel,
        grid=(bsz, ns),
        in_specs=[pl.BlockSpec((tm, W_QKV), tok),
                  pl.BlockSpec((tm, D_IDX_PAD), tok),
                  tab, tab, tab,
                  _const_spec((1, LANES)),
                  _const_spec((1, LANES))],
        out_specs=[tiled(N_HEADS),
                   pl.BlockSpec((None, N_KV_HEADS, tm, HEAD_DIM), lambda b, s: (b, 0, s, 0)),
                   pl.BlockSpec((None, N_KV_HEADS, LANES, tm), lambda b, s: (b, 0, 0, s)),
                   tiled(N_IDX_HEADS),
                   pl.BlockSpec((None, tm, IDX_DIM), lambda b, s: (b, s, 0)),
                   pl.BlockSpec((None, N_IDX_HEADS, tm), lambda b, s: (b, 0, s))],
        out_shape=[jax.ShapeDtypeStruct((bsz, seq // tq, N_HEADS, tq, HEAD_DIM), _MX),
                   jax.ShapeDtypeStruct((bsz, N_KV_HEADS, seq, HEAD_DIM), _MX),
                   jax.ShapeDtypeStruct((bsz, N_KV_HEADS, LANES, seq), _MX),
                   jax.ShapeDtypeStruct((bsz, seq // tq, N_IDX_HEADS, tq, IDX_DIM), _MX),
                   jax.ShapeDtypeStruct((bsz, seq, IDX_DIM), _MX),
                   jax.ShapeDtypeStruct((bsz, N_IDX_HEADS, seq), F32)],
        compiler_params=pltpu.CompilerParams(
            dimension_semantics=("arbitrary", "arbitrary"), vmem_limit_bytes=VMEM_LIMIT),
        name="prep",
    )(qkv, idx, cos, slo, shi, ln_g, ln_b)


_NT = (((1,), (1,)), ((), ()))
REP = N_HEADS // N_KV_HEADS
I16 = jnp.int16
HALF_BITS = 16
HALF_SIGN = 1 << (HALF_BITS - 1)
INT16_MIN = -HALF_SIGN
COUNT_ROWS = 64
IDX_HEADS_PER_DOT = 2


def _attn_kernel(iq_ref, ik_ref, iwt_ref, q_ref, k_ref, vt_ref, o_ref,
                 key_ref, hi_ref, lo_ref, bias_ref, m_ref, acc_ref, jmax_ref, *, topk, ck):
    qt = pl.program_id(1)
    tq = o_ref.shape[0]
    seq = ik_ref.shape[0]
    nck = (qt * tq + tq + ck - 1) // ck
    qpos = qt * tq + lax.broadcasted_iota(I32, (1, tq), 1)
    krow = lax.broadcasted_iota(I32, (ck, 1), 0)
    iwt = iwt_ref[...]

    def fold_rows(x, op):
        y = op(x.reshape(ck // COUNT_ROWS, COUNT_ROWS, x.shape[1]), axis=0).astype(F32)
        y = op(y.reshape(COUNT_ROWS // SUBLANES, SUBLANES, x.shape[1]), axis=0)
        return op(y, axis=0, keepdims=True)

    def score_body(c, _):
        k0 = pl.multiple_of(c * ck, ck)
        ikc = ik_ref[pl.ds(k0, ck), :]
        acc = jnp.zeros((ck, tq), F32)
        for h0 in range(0, N_IDX_HEADS, IDX_HEADS_PER_DOT):
            iq = iq_ref[h0:h0 + IDX_HEADS_PER_DOT].reshape(IDX_HEADS_PER_DOT * tq, IDX_DIM)
            logit = lax.dot_general(ikc, iq, _NT, preferred_element_type=F32)
            for j in range(IDX_HEADS_PER_DOT):
                acc = acc + jnp.maximum(logit[:, j * tq:(j + 1) * tq], 0.0) * iwt[h0 + j:h0 + j + 1, :]
        acc = jnp.where(acc == 0.0, 0.0, acc)
        bits = pltpu.bitcast(acc, I32)
        key = bits ^ ((bits >> 31) & 0x7FFFFFFF)
        key = jnp.where(k0 + krow <= qpos, key, INT_MIN)
        key_ref[pl.ds(k0, ck), :] = key
        hi_ref[pl.ds(k0, ck), :] = (key >> HALF_BITS).astype(I16)
        return 0

    lax.fori_loop(0, nck, score_body, 0)

    def count_over(ref, pred, one, zero, width):
        def body(c, acc):
            k0 = pl.multiple_of(c * ck, ck)
            hit = jnp.where(pred(ref[pl.ds(k0, ck), :], k0), one, zero)
            for j in range(ck // COUNT_ROWS):
                acc = acc + hit[j * COUNT_ROWS:(j + 1) * COUNT_ROWS]
            return acc
        acc = lax.fori_loop(0, nck, body, jnp.zeros((COUNT_ROWS, tq), width))
        return jnp.sum(acc.astype(F32), axis=0, keepdims=True)

    def count(pred):
        return count_over(key_ref, pred, 1.0, 0.0, F32)

    def count_half(ref, pred):
        return count_over(ref, pred, jnp.int16(1), jnp.int16(0), I16)

    def kth_largest_half(ref, k_target, n_start):
        def bit_body(i, carry):
            thr, n_ge = carry
            cand = thr + lax.shift_left(jnp.int32(1), HALF_BITS - 1 - i)
            cand16 = cand.astype(I16)
            cnt = count_half(ref, lambda v, k0: v >= cand16)
            take = cnt >= k_target
            return jnp.where(take, cand, thr), jnp.where(take, cnt, n_ge)
        return lax.fori_loop(0, HALF_BITS, bit_body, (jnp.full((1, tq), INT16_MIN, I32), n_start))

    kk = jnp.minimum(qpos + 1, topk).astype(F32)
    thr_hi, n_ge_hi = kth_largest_half(hi_ref, kk, (qpos + 1).astype(F32))
    thr_hi16 = thr_hi.astype(I16)
    n_gt_hi = count_half(hi_ref, lambda v, k0: v > thr_hi16)

    def low_body(c, _):
        k0 = pl.multiple_of(c * ck, ck)
        low = (key_ref[pl.ds(k0, ck), :] ^ HALF_SIGN).astype(I16)
        lo_ref[pl.ds(k0, ck), :] = jnp.where(hi_ref[pl.ds(k0, ck), :] == thr_hi16, low, jnp.int16(INT16_MIN))
        return 0

    lax.fori_loop(0, nck, low_body, 0)
    thr_lo, n_ge_lo = kth_largest_half(lo_ref, kk - n_gt_hi, n_ge_hi - n_gt_hi)
    thr = thr_hi * (1 << HALF_BITS) + (thr_lo - INT16_MIN)
    n_ge = n_gt_hi + n_ge_lo

    jmax_ref[...] = jnp.full(jmax_ref.shape, seq, I32)

    @pl.when(jnp.max(n_ge - kk) > 0.0)
    def _():
        need = kk - count(lambda keys, k0: keys > thr)
        nbits = max(1, (seq - 1).bit_length())

        def jbit_body(i, x):
            cand = x + lax.shift_left(jnp.int32(1), nbits - 1 - i)
            below = count(lambda keys, k0: (keys == thr) & (k0 + krow < cand))
            return jnp.where(below < need, cand, x)

        x = lax.fori_loop(0, nbits, jbit_body, jnp.zeros((1, tq), I32))
        jmax_ref[...] = jnp.broadcast_to(x, jmax_ref.shape)

    jmax = jmax_ref[0:1, :]

    def bias_body(c, _):
        k0 = pl.multiple_of(c * ck, ck)
        keys = key_ref[pl.ds(k0, ck), :]
        kpos = k0 + krow
        sel = ((keys > thr) | ((keys == thr) & (kpos <= jmax))) & (kpos <= qpos)
        bias_ref[pl.ds(k0, ck), :] = jnp.where(sel, 0.0, NEG_BIG).astype(bias_ref.dtype)
        return 0

    lax.fori_loop(0, nck, bias_body, 0)

    m_ref[...] = jnp.full(m_ref.shape, NEG_BIG, F32)
    acc_ref[...] = jnp.zeros(acc_ref.shape, F32)

    def attn_body(c, _):
        k0 = pl.multiple_of(c * ck, ck)
        bias = bias_ref[pl.ds(k0, ck), :]

        def logits(g):
            q = q_ref[g * REP:(g + 1) * REP].reshape(REP * tq, HEAD_DIM)
            return lax.dot_general(k_ref[g, pl.ds(k0, ck), :], q, _NT, preferred_element_type=F32)

        s_next = logits(0)
        for g in range(N_KV_HEADS):
            s = s_next.astype(_MX)
            if g + 1 < N_KV_HEADS:
                s_next = logits(g + 1)
            s = jnp.concatenate([s[:, r * tq:(r + 1) * tq] + bias for r in range(REP)], axis=1)
            m_old = m_ref[g]
            m_new = jnp.maximum(m_old, fold_rows(s, jnp.max))
            m_ref[g] = m_new
            p = jnp.exp2(s - m_new.astype(_MX))
            acc_ref[g] = jnp.exp2(m_old - m_new) * acc_ref[g] + jnp.dot(
                vt_ref[g, :, pl.ds(k0, ck)], p, preferred_element_type=F32)
        return 0

    lax.fori_loop(0, nck, attn_body, 0)

    for g in range(N_KV_HEADS):
        acc = acc_ref[g]
        out = acc[:HEAD_DIM, :] / acc[HEAD_DIM:HEAD_DIM + 1, :]
        for r in range(0, REP, 2):
            pair = jnp.concatenate([out[:, r * tq:(r + 1) * tq], out[:, (r + 1) * tq:(r + 2) * tq]], axis=0)
            h = g * REP + r
            o_ref[:, h * HEAD_DIM:(h + 2) * HEAD_DIM] = pair.T.astype(o_ref.dtype)


def _attn(iq, ik, iwt, q, k, vt, bsz, seq):
    topk = min(TOPK_MAX, seq // 4)
    tq = min(TQ, seq)
    ck = min(CK, seq)
    nq = seq // tq
    tile = lambda nh: pl.BlockSpec((None, None, nh, tq, HEAD_DIM), lambda b, t: (b, t, 0, 0, 0))
    return pl.pallas_call(
        functools.partial(_attn_kernel, topk=topk, ck=ck),
        grid=(bsz, nq),
        in_specs=[tile(N_IDX_HEADS),
                  pl.BlockSpec((None, seq, IDX_DIM), lambda b, t: (b, 0, 0)),
                  pl.BlockSpec((None, N_IDX_HEADS, tq), lambda b, t: (b, 0, t)),
                  tile(N_HEADS),
                  pl.BlockSpec((None, N_KV_HEADS, seq, HEAD_DIM), lambda b, t: (b, 0, 0, 0)),
                  pl.BlockSpec((None, N_KV_HEADS, LANES, seq), lambda b, t: (b, 0, 0, 0))],
        out_specs=pl.BlockSpec((tq, D_Q), lambda b, t: (b * nq + t, 0)),
        out_shape=jax.ShapeDtypeStruct((bsz * seq, D_Q), _MX),
        scratch_shapes=[pltpu.VMEM((seq, tq), I32),
                        pltpu.VMEM((seq, tq), I16),
                        pltpu.VMEM((seq, tq), I16),
                        pltpu.VMEM((seq, tq), _MX),
                        pltpu.VMEM((N_KV_HEADS, 1, REP * tq), F32),
                        pltpu.VMEM((N_KV_HEADS, LANES, REP * tq), F32),
                        pltpu.VMEM((SUBLANES, tq), I32)],
        compiler_params=pltpu.CompilerParams(
            dimension_semantics=("arbitrary", "arbitrary"), vmem_limit_bytes=VMEM_LIMIT),
        name="attn",
    )(iq, ik, iwt, q, k, vt)


def _merge_kernel(x_ref, yr_ref, ya_ref, gate_ref, wr_ref, wa_ref, wo_ref, g_ref, o_ref):
    y_a = jnp.dot(yr_ref[...], wr_ref[...], preferred_element_type=F32)
    y_b = jnp.dot(ya_ref[...], wa_ref[...], preferred_element_type=F32)
    g_a = jax.nn.sigmoid(gate_ref[:, :D_MODEL].astype(F32))
    g_b = jax.nn.sigmoid(gate_ref[:, D_MODEL:].astype(F32))
    merged = g_a * y_a + g_b * y_b
    mix = jnp.dot(merged.astype(_MX), wo_ref[...], preferred_element_type=F32)
    o_ref[...] = x_ref[...] + _rms(mix, g_ref[...])


def _merge(x2, y_rnn, y_att, gates, w_rnn_out, w_att_out, w_o, g):
    n = x2.shape[0]
    row = lambda w: pl.BlockSpec((TM, w), lambda i: (i, 0))
    return pl.pallas_call(
        _merge_kernel,
        grid=(n // TM,),
        in_specs=[row(D_MODEL), row(D_RNN), row(D_Q), row(W_GATE),
                  _const_spec((D_RNN, D_MODEL)), _const_spec((D_Q, D_MODEL)),
                  _const_spec((D_MODEL, D_MODEL)), _const_spec((1, D_MODEL))],
        out_specs=row(D_MODEL),
        out_shape=jax.ShapeDtypeStruct((n, D_MODEL), F32),
        compiler_params=pltpu.CompilerParams(
            dimension_semantics=("arbitrary",), vmem_limit_bytes=VMEM_LIMIT),
        name="merge",
    )(x2, y_rnn, y_att, gates, w_rnn_out, w_att_out, w_o, g)


FF_CHUNK = D_FF // 2


def _ffn_kernel(x_ref, gpre_ref, wg_ref, wu_ref, wd_ref, gpost_ref, o_ref):
    x = x_ref[...]
    h = _rms(x, gpre_ref[...]).astype(_MX)
    f = jnp.zeros(x.shape, F32)
    for c in range(0, D_FF, FF_CHUNK):
        gate = jnp.dot(h, wg_ref[:, c:c + FF_CHUNK], preferred_element_type=F32)
        up = jnp.dot(h, wu_ref[:, c:c + FF_CHUNK], preferred_element_type=F32)
        act = (jax.nn.silu(gate) * up).astype(_MX)
        f = f + jnp.dot(act, wd_ref[c:c + FF_CHUNK, :], preferred_element_type=F32)
    o_ref[...] = x + _rms(f, gpost_ref[...])


def _ffn(x1, g_pre, w_gate, w_up, w_down, g_post):
    n = x1.shape[0]
    row = pl.BlockSpec((TM, D_MODEL), lambda i: (i, 0))
    return pl.pallas_call(
        _ffn_kernel,
        grid=(n // TM,),
        in_specs=[row, _const_spec((1, D_MODEL)),
                  _const_spec((D_MODEL, D_FF)), _const_spec((D_MODEL, D_FF)),
                  _const_spec((D_FF, D_MODEL)), _const_spec((1, D_MODEL))],
        out_specs=row,
        out_shape=jax.ShapeDtypeStruct((n, D_MODEL), F32),
        compiler_params=pltpu.CompilerParams(
            dimension_semantics=("arbitrary",), vmem_limit_bytes=VMEM_LIMIT),
        name="ffn",
    )(x1, g_pre, w_gate, w_up, w_down, g_post)


def _rope_tables(seq):
    half = ROT_DIM // 2
    inv_freq = ROPE_THETA ** (-jnp.arange(half, dtype=F32) / half)
    ang = jnp.arange(seq, dtype=F32)[:, None] * inv_freq[None, :]
    cos, sin = jnp.cos(ang), jnp.sin(ang)
    rest = HEAD_DIM - ROT_DIM
    ones, zeros, zh = jnp.ones((seq, rest), F32), jnp.zeros((seq, rest), F32), jnp.zeros((seq, half), F32)
    cos_t = jnp.concatenate([cos, cos, ones], axis=1)
    slo_t = jnp.concatenate([-sin, zh, zeros], axis=1)
    shi_t = jnp.concatenate([zh, sin, zeros], axis=1)
    two = lambda t: jnp.concatenate([t, t], axis=1)
    return two(cos_t), two(slo_t), two(shi_t)


def _gate_weights(w_a, w_x):
    per = GATE_GROUP // RNN_BLOCK

    def diag(w):
        w = w.reshape(D_RNN // GATE_GROUP, per, RNN_BLOCK, RNN_BLOCK)
        eye = jnp.eye(per, dtype=w.dtype)
        return jnp.einsum('gpij,pq->gpiqj', w, eye).reshape(-1, GATE_GROUP, GATE_GROUP)

    return jnp.concatenate([diag(w_a), diag(w_x)], axis=-1).astype(_MX)


def kernel(x, norm_mix_pre, w_in, conv_w, conv_b, rg_w_a, rg_b_a, rg_w_x, rg_b_x, rg_lambda,
           idx_k_ln_g, idx_k_ln_b, w_rnn_out, w_att_out, w_o, norm_mix_post,
           norm_ffn_pre, w_ffn_gate, w_ffn_up, w_ffn_down, norm_ffn_post):
    bsz, seq, _ = x.shape
    depth = w_in.shape[0]
    x2 = x.reshape(bsz * seq, D_MODEL)
    cos, slo, shi = _rope_tables(seq)
    o_idx = W_RNN + W_QKV
    for l in range(depth):
        w = w_in[l]
        w_all = jnp.concatenate(
            [w[:, :o_idx], w[:, o_idx + D_IDX:], w[:, o_idx:o_idx + D_IDX],
             jnp.zeros((D_MODEL, D_IDX_PAD - D_IDX), w.dtype)], axis=1).astype(_MX)
        rnn, qkv, gates, idx = _inproj(x2, norm_mix_pre[l][None], w_all)
        y_rnn = _rnn(rnn, conv_w[l], conv_b[l][None], _gate_weights(rg_w_a[l], rg_w_x[l]),
                     rg_b_a[l][None], rg_b_x[l][None], rg_lambda[l][None], bsz, seq)
        pad_lanes = lambda t: jnp.pad(t, (0, LANES - IDX_DIM))[None]
        q, k, vt, iq, ik, iwt = _prep(qkv, idx, cos, slo, shi,
                                      pad_lanes(idx_k_ln_g[l]), pad_lanes(idx_k_ln_b[l]), bsz, seq)
        y_att = _attn(iq, ik, iwt, q, k, vt, bsz, seq)
        x1 = _merge(x2, y_rnn, y_att, gates, w_rnn_out[l].astype(_MX), w_att_out[l].astype(_MX),
                    w_o[l].astype(_MX), norm_mix_post[l][None])
        x2 = _ffn(x1, norm_ffn_pre[l][None], w_ffn_gate[l].astype(_MX), w_ffn_up[l].astype(_MX),
                  w_ffn_down[l].astype(_MX), norm_ffn_post[l][None])
    return x2.reshape(bsz, seq, D_MODEL)
```

```python
import functools
import math

import jax
import jax.numpy as jnp
from jax import lax
from jax.experimental import pallas as pl
from jax.experimental.pallas import tpu as pltpu

F32 = jnp.float32
I32 = jnp.int32
_MX = jnp.bfloat16

D_MODEL = 1024
D_RNN = 1024
N_RNN_BLOCKS = 16
RNN_BLOCK = D_RNN // N_RNN_BLOCKS
CONV_WIDTH = 4
LRU_C = 8.0
N_HEADS = 16
N_KV_HEADS = 4
HEAD_DIM = 64
ROT_DIM = HEAD_DIM // 4
ROPE_THETA = 500000.0
N_IDX_HEADS = 8
IDX_DIM = 64
TOPK_MAX = 256
D_FF = 2816
EPS = 1e-6

D_Q = N_HEADS * HEAD_DIM
D_KV = N_KV_HEADS * HEAD_DIM
D_QKV = D_Q + 2 * D_KV
D_IQ = N_IDX_HEADS * IDX_DIM
D_IDX = D_IQ + IDX_DIM + N_IDX_HEADS
D_IDX_PAD = 640
W_RNN, W_QKV, W_GATE = 2 * D_RNN, D_QKV, 2 * D_MODEL
W_ALL = W_RNN + W_QKV + W_GATE + D_IDX_PAD

LANES = 128
SUBLANES = 8
VMEM_LIMIT = 56 * 1024 * 1024
NEG_BIG = -1e30
INT_MIN = -(2 ** 31)

TM = 512
TS = 512
TQ = 256
CK = 512
GATE_GROUP = 256


def _const_spec(shape):
    nd = len(shape)
    return pl.BlockSpec(shape, lambda *_: (0,) * nd, pipeline_mode=pl.Buffered(1))


def _rms(x, g):
    return x * lax.rsqrt(jnp.mean(x * x, axis=-1, keepdims=True) + EPS) * g


def _sigmoid(x):
    return 0.5 * jnp.tanh(0.5 * x) + 0.5


def _project(h, w_ref, ref, col, width):
    step = 512 if width % 512 == 0 else width
    for c in range(0, width, step):
        ref[:, c:c + step] = jnp.dot(
            h, w_ref[:, col + c:col + c + step], preferred_element_type=F32).astype(ref.dtype)


def _rnn_carry_in(s, xpad_ref, hc_ref):
    ts = xpad_ref.shape[0] - SUBLANES

    @pl.when(s == 0)
    def _():
        xpad_ref[0:SUBLANES, :] = jnp.zeros((SUBLANES, D_RNN), F32)
        hc_ref[...] = jnp.zeros_like(hc_ref)

    @pl.when(s != 0)
    def _():
        xpad_ref[0:SUBLANES, :] = xpad_ref[ts:ts + SUBLANES, :]


def _rnn_gates(rnn_ref, cw_ref, cb_ref, wg_ref, ba_ref, bx_ref, lam_ref, xpad_ref, a_ref, h_ref, alongside):
    ts = rnn_ref.shape[0]
    xpad_ref[SUBLANES:SUBLANES + ts, :] = rnn_ref[:, 0:D_RNN].astype(F32)

    nl = -lam_ref[...]
    softplus_nl = jnp.maximum(nl, 0.0) + jnp.log1p(jnp.exp(-jnp.abs(nl)))

    for c in range(0, D_RNN, GATE_GROUP):
        alongside[c // GATE_GROUP]()
        cs = slice(c, c + GATE_GROUP)
        xc = cb_ref[:, cs]
        for k in range(CONV_WIDTH):
            off = SUBLANES - (CONV_WIDTH - 1) + k
            xc = xc + xpad_ref[off:off + ts, cs] * cw_ref[k:k + 1, cs]
        gz = jnp.dot(xc.astype(_MX), wg_ref[c // GATE_GROUP], preferred_element_type=F32)
        r = _sigmoid(gz[:, :GATE_GROUP] + ba_ref[:, cs])
        i = _sigmoid(gz[:, GATE_GROUP:] + bx_ref[:, cs])
        log_a = (-LRU_C * r) * softplus_nl[:, cs]
        a = jnp.exp(log_a)
        u = jnp.sqrt(-jnp.tanh(log_a) * (a * a + 1.0)) * (i * xc)
        for j in range(GATE_GROUP // LANES):
            a_ref[c // LANES + j] = a[:, j * LANES:(j + 1) * LANES]
            h_ref[c // LANES + j] = u[:, j * LANES:(j + 1) * LANES]


def _rnn_scan(a_ref, h_ref, sp_ref, sh_ref, cin_ref, hc_ref):
    ts = h_ref.shape[1]
    ng = ts // SUBLANES

    for j in range(D_RNN // LANES):
        ls = slice(j * LANES, (j + 1) * LANES)
        h = h_ref[j, pl.ds(0, ng, stride=SUBLANES), :]
        p = a_ref[j, pl.ds(0, ng, stride=SUBLANES), :]
        for i in range(1, SUBLANES):
            a_i = a_ref[j, pl.ds(i, ng, stride=SUBLANES), :]
            h = a_i * h + h_ref[j, pl.ds(i, ng, stride=SUBLANES), :]
            p = a_i * p
            h_ref[j, pl.ds(i, ng, stride=SUBLANES), :] = h
            a_ref[j, pl.ds(i, ng, stride=SUBLANES), :] = p
        sp_ref[:, ls] = p
        sh_ref[:, ls] = h

    def carry_body(g, carry):
        cin_ref[pl.ds(g, 1), :] = carry
        return sp_ref[pl.ds(g, 1), :] * carry + sh_ref[pl.ds(g, 1), :]

    hc_ref[0:1, :] = lax.fori_loop(0, ng, carry_body, hc_ref[0:1, :])

    def out_body(g, _):
        r0 = pl.multiple_of(g * SUBLANES, SUBLANES)
        rows = pl.ds(r0, SUBLANES)
        cin = cin_ref[pl.ds(g, 1), :]
        for j in range(D_RNN // LANES):
            cj = jnp.broadcast_to(cin[:, j * LANES:(j + 1) * LANES], (SUBLANES, LANES))
            h_ref[j, rows, :] = h_ref[j, rows, :] + a_ref[j, rows, :] * cj
        return 0

    lax.fori_loop(0, ng, out_body, 0)


def _rnn_out(rnn_ref, h_ref, y_ref):
    for j in range(D_RNN // LANES):
        gr = rnn_ref[:, D_RNN + j * LANES:D_RNN + (j + 1) * LANES].astype(F32)
        y_ref[:, j * LANES:(j + 1) * LANES] = (
            h_ref[j] * jax.nn.gelu(gr, approximate=True)).astype(y_ref.dtype)


def _rope_block(x, cos, sin_lo, sin_hi):
    w = x.shape[1]
    half = ROT_DIM // 2
    return x * cos + pltpu.roll(x, w - half, 1) * sin_lo + pltpu.roll(x, half, 1) * sin_hi


def _prep(qkv_ref, idx_ref, cos_ref, slo_ref, shi_ref, lng_ref, lnb_ref,
          q_ref, k_ref, vt_ref, iq_ref, ik_ref, iwt_ref):
    cos, slo, shi = cos_ref[...], slo_ref[...], shi_ref[...]
    tm = qkv_ref.shape[0]
    tq = q_ref.shape[2]
    scale = (HEAD_DIM ** -0.5) * math.log2(math.e)
    heads_per_blk = LANES // HEAD_DIM

    def put_heads(ref, blk, rb):
        for j in range(heads_per_blk):
            for t in range(tm // tq):
                ref[t, blk * heads_per_blk + j] = rb[t * tq:(t + 1) * tq, j * HEAD_DIM:(j + 1) * HEAD_DIM]

    for blk in range(D_Q // LANES):
        xb = qkv_ref[:, blk * LANES:(blk + 1) * LANES].astype(F32) * scale
        put_heads(q_ref, blk, _rope_block(xb, cos, slo, shi).astype(q_ref.dtype))
    ones_col = (lax.broadcasted_iota(I32, (tm, HEAD_DIM), 1) == 0).astype(F32)
    for blk in range(D_KV // LANES):
        c0 = D_Q + blk * LANES
        rb = _rope_block(qkv_ref[:, c0:c0 + LANES].astype(F32), cos, slo, shi).astype(k_ref.dtype)
        for j in range(heads_per_blk):
            k_ref[blk * heads_per_blk + j] = rb[:, j * HEAD_DIM:(j + 1) * HEAD_DIM]
            v0 = D_Q + D_KV + blk * LANES + j * HEAD_DIM
            v_aug = jnp.concatenate([qkv_ref[:, v0:v0 + HEAD_DIM].astype(F32), ones_col], axis=1)
            vt_ref[blk * heads_per_blk + j] = v_aug.T.astype(vt_ref.dtype)
    for blk in range(D_IQ // LANES):
        rb = _rope_block(idx_ref[:, blk * LANES:(blk + 1) * LANES], cos, slo, shi).astype(iq_ref.dtype)
        put_heads(iq_ref, blk, rb)
    tail = idx_ref[:, D_IQ:D_IQ + LANES]
    is_key = lax.broadcasted_iota(I32, (1, LANES), 1) < IDX_DIM
    mu = jnp.sum(jnp.where(is_key, tail, 0.0), axis=-1, keepdims=True) * (1.0 / IDX_DIM)
    cen = jnp.where(is_key, tail - mu, 0.0)
    var = jnp.sum(cen * cen, axis=-1, keepdims=True) * (1.0 / IDX_DIM)
    ik = cen * lax.rsqrt(var + EPS) * lng_ref[...] + lnb_ref[...]
    ik = _rope_block(ik, cos, slo, shi)
    ik_ref[...] = ik[:, :IDX_DIM].astype(ik_ref.dtype)
    iwt_ref[...] = tail.T[IDX_DIM:IDX_DIM + N_IDX_HEADS, :] * ((N_IDX_HEADS ** -0.5) * (IDX_DIM ** -0.5))


COL_RNN, COL_QKV, COL_GATE, COL_IDX = 0, W_RNN, W_RNN + W_QKV, W_RNN + W_QKV + W_GATE


def _front_kernel(x_ref, g_ref, w_ref, cw_ref, cb_ref, wg_ref, ba_ref, bx_ref, lam_ref,
                  cos_ref, slo_ref, shi_ref, lng_ref, lnb_ref,
                  y_ref, gate_ref, q_ref, k_ref, vt_ref, iq_ref, ik_ref, iwt_ref,
                  rnn_s, qkv_s, idx_s, xpad_ref, a_ref, h_ref, sp_ref, sh_ref, cin_ref, hc_ref, *, ns):
    _rnn_carry_in(pl.program_id(0) % ns, xpad_ref, hc_ref)
    h = _rms(x_ref[...], g_ref[...]).astype(_MX)
    _project(h, w_ref, rnn_s, COL_RNN, W_RNN)
    qkv_part = W_QKV // 3
    alongside = [functools.partial(_project, h, w_ref, qkv_s.at[:, i * qkv_part:(i + 1) * qkv_part],
                                   COL_QKV + i * qkv_part, qkv_part) for i in range(3)]
    alongside.append(functools.partial(_project, h, w_ref, idx_s, COL_IDX, D_IDX_PAD))
    _rnn_gates(rnn_s, cw_ref, cb_ref, wg_ref, ba_ref, bx_ref, lam_ref, xpad_ref, a_ref, h_ref, alongside)
    _rnn_scan(a_ref, h_ref, sp_ref, sh_ref, cin_ref, hc_ref)
    _project(h, w_ref, gate_ref, COL_GATE, W_GATE)
    _rnn_out(rnn_s, h_ref, y_ref)
    _prep(qkv_s, idx_s, cos_ref, slo_ref, shi_ref, lng_ref, lnb_ref,
          q_ref, k_ref, vt_ref, iq_ref, ik_ref, iwt_ref)


def _front(x2, g, w_all, conv_w, conv_b, w_gates, b_a, b_x, lam, cos, slo, shi, ln_g, ln_b, bsz, seq):
    n = x2.shape[0]
    tm = min(TM, seq)
    tq = min(TQ, seq)
    ns = seq // tm
    ng = tm // SUBLANES
    tok = lambda w: pl.BlockSpec((tm, w), lambda i: (i, 0))
    tab = pl.BlockSpec((tm, LANES), lambda i: (i % ns, 0))
    tiled = lambda nh: pl.BlockSpec((None, tm // tq, nh, tq, HEAD_DIM), lambda i: (i // ns, i % ns, 0, 0, 0))
    return pl.pallas_call(
        functools.partial(_front_kernel, ns=ns),
        grid=(n // tm,),
        in_specs=[tok(D_MODEL), _const_spec((1, D_MODEL)), _const_spec((D_MODEL, W_ALL)),
                  _const_spec((CONV_WIDTH, D_RNN)), _const_spec((1, D_RNN)),
                  _const_spec((D_RNN // GATE_GROUP, GATE_GROUP, 2 * GATE_GROUP)),
                  _const_spec((1, D_RNN)), _const_spec((1, D_RNN)), _const_spec((1, D_RNN)),
                  tab, tab, tab, _const_spec((1, LANES)), _const_spec((1, LANES))],
        out_specs=[tok(D_RNN), tok(W_GATE),
                   tiled(N_HEADS),
                   pl.BlockSpec((None, N_KV_HEADS, tm, HEAD_DIM), lambda i: (i // ns, 0, i % ns, 0)),
                   pl.BlockSpec((None, N_KV_HEADS, LANES, tm), lambda i: (i // ns, 0, 0, i % ns)),
                   tiled(N_IDX_HEADS),
                   pl.BlockSpec((None, tm, IDX_DIM), lambda i: (i // ns, i % ns, 0)),
                   pl.BlockSpec((None, N_IDX_HEADS, tm), lambda i: (i // ns, 0, i % ns))],
        out_shape=[jax.ShapeDtypeStruct((n, D_RNN), _MX),
                   jax.ShapeDtypeStruct((n, W_GATE), _MX),
                   jax.ShapeDtypeStruct((bsz, seq // tq, N_HEADS, tq, HEAD_DIM), _MX),
                   jax.ShapeDtypeStruct((bsz, N_KV_HEADS, seq, HEAD_DIM), _MX),
                   jax.ShapeDtypeStruct((bsz, N_KV_HEADS, LANES, seq), _MX),
                   jax.ShapeDtypeStruct((bsz, seq // tq, N_IDX_HEADS, tq, IDX_DIM), _MX),
                   jax.ShapeDtypeStruct((bsz, seq, IDX_DIM), _MX),
                   jax.ShapeDtypeStruct((bsz, N_IDX_HEADS, seq), F32)],
        scratch_shapes=[pltpu.VMEM((tm, W_RNN), _MX),
                        pltpu.VMEM((tm, W_QKV), _MX),
                        pltpu.VMEM((tm, D_IDX_PAD), F32),
                        pltpu.VMEM((tm + SUBLANES, D_RNN), F32),
                        pltpu.VMEM((D_RNN // LANES, tm, LANES), F32),
                        pltpu.VMEM((D_RNN // LANES, tm, LANES), F32),
                        pltpu.VMEM((ng, D_RNN), F32),
                        pltpu.VMEM((ng, D_RNN), F32),
                        pltpu.VMEM((ng, D_RNN), F32),
                        pltpu.VMEM((SUBLANES, D_RNN), F32)],
        compiler_params=pltpu.CompilerParams(
            dimension_semantics=("arbitrary",), vmem_limit_bytes=VMEM_LIMIT),
        name="front",
    )(x2, g, w_all, conv_w, conv_b, w_gates, b_a, b_x, lam, cos, slo, shi, ln_g, ln_b)


_NT = (((1,), (1,)), ((), ()))
REP = N_HEADS // N_KV_HEADS
I16 = jnp.int16
HALF_BITS = 16
HALF_SIGN = 1 << (HALF_BITS - 1)
INT16_MIN = -HALF_SIGN
COUNT_ROWS = 64
IDX_HEADS_PER_DOT = 2


def _attn_kernel(iq_ref, ik_ref, iwt_ref, q_ref, k_ref, vt_ref, o_ref,
                 key_ref, hi_ref, lo_ref, bias_ref, m_ref, acc_ref, jmax_ref, *, topk, ck):
    qt = pl.program_id(1)
    tq = o_ref.shape[0]
    seq = ik_ref.shape[0]
    nck = (qt * tq + tq + ck - 1) // ck
    qpos = qt * tq + lax.broadcasted_iota(I32, (1, tq), 1)
    krow = lax.broadcasted_iota(I32, (ck, 1), 0)
    iwt = iwt_ref[...]

    def fold_rows(x, op):
        y = op(x.reshape(ck // COUNT_ROWS, COUNT_ROWS, x.shape[1]), axis=0).astype(F32)
        y = op(y.reshape(COUNT_ROWS // SUBLANES, SUBLANES, x.shape[1]), axis=0)
        return op(y, axis=0, keepdims=True)

    def score_body(c, _):
        k0 = pl.multiple_of(c * ck, ck)
        ikc = ik_ref[pl.ds(k0, ck), :]
        acc = jnp.zeros((ck, tq), F32)
        for h0 in range(0, N_IDX_HEADS, IDX_HEADS_PER_DOT):
            iq = iq_ref[h0:h0 + IDX_HEADS_PER_DOT].reshape(IDX_HEADS_PER_DOT * tq, IDX_DIM)
            logit = lax.dot_general(ikc, iq, _NT, preferred_element_type=F32)
            for j in range(IDX_HEADS_PER_DOT):
                acc = acc + jnp.maximum(logit[:, j * tq:(j + 1) * tq], 0.0) * iwt[h0 + j:h0 + j + 1, :]
        acc = jnp.where(acc == 0.0, 0.0, acc)
        bits = pltpu.bitcast(acc, I32)
        key = bits ^ ((bits >> 31) & 0x7FFFFFFF)
        key = jnp.where(k0 + krow <= qpos, key, INT_MIN)
        key_ref[pl.ds(k0, ck), :] = key
        hi_ref[pl.ds(k0, ck), :] = (key >> HALF_BITS).astype(I16)
        return 0

    lax.fori_loop(0, nck, score_body, 0)

    def count_over(ref, pred, one, zero, width):
        def body(c, acc):
            k0 = pl.multiple_of(c * ck, ck)
            hit = jnp.where(pred(ref[pl.ds(k0, ck), :], k0), one, zero)
            for j in range(ck // COUNT_ROWS):
                acc = acc + hit[j * COUNT_ROWS:(j + 1) * COUNT_ROWS]
            return acc
        acc = lax.fori_loop(0, nck, body, jnp.zeros((COUNT_ROWS, tq), width))
        return jnp.sum(acc.astype(F32), axis=0, keepdims=True)

    def count(pred):
        return count_over(key_ref, pred, 1.0, 0.0, F32)

    def count_half(ref, pred):
        return count_over(ref, pred, jnp.int16(1), jnp.int16(0), I16)

    def kth_largest_half(ref, k_target, n_start):
        def bit_body(i, carry):
            thr, n_ge = carry
            cand = thr + lax.shift_left(jnp.int32(1), HALF_BITS - 1 - i)
            cand16 = cand.astype(I16)
            cnt = count_half(ref, lambda v, k0: v >= cand16)
            take = cnt >= k_target
            return jnp.where(take, cand, thr), jnp.where(take, cnt, n_ge)
        return lax.fori_loop(0, HALF_BITS, bit_body, (jnp.full((1, tq), INT16_MIN, I32), n_start))

    kk = jnp.minimum(qpos + 1, topk).astype(F32)
    thr_hi, n_ge_hi = kth_largest_half(hi_ref, kk, (qpos + 1).astype(F32))
    thr_hi16 = thr_hi.astype(I16)
    n_gt_hi = count_half(hi_ref, lambda v, k0: v > thr_hi16)

    def low_body(c, _):
        k0 = pl.multiple_of(c * ck, ck)
        low = (key_ref[pl.ds(k0, ck), :] ^ HALF_SIGN).astype(I16)
        lo_ref[pl.ds(k0, ck), :] = jnp.where(hi_ref[pl.ds(k0, ck), :] == thr_hi16, low, jnp.int16(INT16_MIN))
        return 0

    lax.fori_loop(0, nck, low_body, 0)
    thr_lo, n_ge_lo = kth_largest_half(lo_ref, kk - n_gt_hi, n_ge_hi - n_gt_hi)
    thr = thr_hi * (1 << HALF_BITS) + (thr_lo - INT16_MIN)
    n_ge = n_gt_hi + n_ge_lo

    jmax_ref[...] = jnp.full(jmax_ref.shape, seq, I32)

    @pl.when(jnp.max(n_ge - kk) > 0.0)
    def _():
        need = kk - count(lambda keys, k0: keys > thr)
        nbits = max(1, (seq - 1).bit_length())

        def jbit_body(i, x):
            cand = x + lax.shift_left(jnp.int32(1), nbits - 1 - i)
            below = count(lambda keys, k0: (keys == thr) & (k0 + krow < cand))
            return jnp.where(below < need, cand, x)

        x = lax.fori_loop(0, nbits, jbit_body, jnp.zeros((1, tq), I32))
        jmax_ref[...] = jnp.broadcast_to(x, jmax_ref.shape)

    jmax = jmax_ref[0:1, :]

    def bias_body(c, _):
        k0 = pl.multiple_of(c * ck, ck)
        keys = key_ref[pl.ds(k0, ck), :]
        kpos = k0 + krow
        sel = ((keys > thr) | ((keys == thr) & (kpos <= jmax))) & (kpos <= qpos)
        bias_ref[pl.ds(k0, ck), :] = jnp.where(sel, 0.0, NEG_BIG).astype(bias_ref.dtype)
        return 0

    lax.fori_loop(0, nck, bias_body, 0)

    m_ref[...] = jnp.full(m_ref.shape, NEG_BIG, F32)
    acc_ref[...] = jnp.zeros(acc_ref.shape, F32)

    def attn_body(c, _):
        k0 = pl.multiple_of(c * ck, ck)
        bias = bias_ref[pl.ds(k0, ck), :]

        def logits(g):
            q = q_ref[g * REP:(g + 1) * REP].reshape(REP * tq, HEAD_DIM)
            return lax.dot_general(k_ref[g, pl.ds(k0, ck), :], q, _NT, preferred_element_type=F32)

        s_next = logits(0)
        for g in range(N_KV_HEADS):
            s = s_next.astype(_MX)
            if g + 1 < N_KV_HEADS:
                s_next = logits(g + 1)
            s = jnp.concatenate([s[:, r * tq:(r + 1) * tq] + bias for r in range(REP)], axis=1)
            m_old = m_ref[g]
            m_new = jnp.maximum(m_old, fold_rows(s, jnp.max))
            m_ref[g] = m_new
            p = jnp.exp2(s - m_new.astype(_MX))
            acc_ref[g] = jnp.exp2(m_old - m_new) * acc_ref[g] + jnp.dot(
                vt_ref[g, :, pl.ds(k0, ck)], p, preferred_element_type=F32)
        return 0

    lax.fori_loop(0, nck, attn_body, 0)

    for g in range(N_KV_HEADS):
        acc = acc_ref[g]
        out = acc[:HEAD_DIM, :] / acc[HEAD_DIM:HEAD_DIM + 1, :]
        for r in range(0, REP, 2):
            pair = jnp.concatenate([out[:, r * tq:(r + 1) * tq], out[:, (r + 1) * tq:(r + 2) * tq]], axis=0)
            h = g * REP + r
            o_ref[:, h * HEAD_DIM:(h + 2) * HEAD_DIM] = pair.T.astype(o_ref.dtype)


def _attn(iq, ik, iwt, q, k, vt, bsz, seq):
    topk = min(TOPK_MAX, seq // 4)
    tq = min(TQ, seq)
    ck = min(CK, seq)
    nq = seq // tq
    tile = lambda nh: pl.BlockSpec((None, None, nh, tq, HEAD_DIM), lambda b, t: (b, t, 0, 0, 0))
    return pl.pallas_call(
        functools.partial(_attn_kernel, topk=topk, ck=ck),
        grid=(bsz, nq),
        in_specs=[tile(N_IDX_HEADS),
                  pl.BlockSpec((None, seq, IDX_DIM), lambda b, t: (b, 0, 0)),
                  pl.BlockSpec((None, N_IDX_HEADS, tq), lambda b, t: (b, 0, t)),
                  tile(N_HEADS),
                  pl.BlockSpec((None, N_KV_HEADS, seq, HEAD_DIM), lambda b, t: (b, 0, 0, 0)),
                  pl.BlockSpec((None, N_KV_HEADS, LANES, seq), lambda b, t: (b, 0, 0, 0))],
        out_specs=pl.BlockSpec((tq, D_Q), lambda b, t: (b * nq + t, 0)),
        out_shape=jax.ShapeDtypeStruct((bsz * seq, D_Q), _MX),
        scratch_shapes=[pltpu.VMEM((seq, tq), I32),
                        pltpu.VMEM((seq, tq), I16),
                        pltpu.VMEM((seq, tq), I16),
                        pltpu.VMEM((seq, tq), _MX),
                        pltpu.VMEM((N_KV_HEADS, 1, REP * tq), F32),
                        pltpu.VMEM((N_KV_HEADS, LANES, REP * tq), F32),
                        pltpu.VMEM((SUBLANES, tq), I32)],
        compiler_params=pltpu.CompilerParams(
            dimension_semantics=("arbitrary", "arbitrary"), vmem_limit_bytes=VMEM_LIMIT),
        name="attn",
    )(iq, ik, iwt, q, k, vt)


def _merge_kernel(x_ref, yr_ref, ya_ref, gate_ref, wr_ref, wa_ref, wo_ref, g_ref, o_ref):
    y_a = jnp.dot(yr_ref[...], wr_ref[...], preferred_element_type=F32)
    y_b = jnp.dot(ya_ref[...], wa_ref[...], preferred_element_type=F32)
    g_a = jax.nn.sigmoid(gate_ref[:, :D_MODEL].astype(F32))
    g_b = jax.nn.sigmoid(gate_ref[:, D_MODEL:].astype(F32))
    merged = g_a * y_a + g_b * y_b
    mix = jnp.dot(merged.astype(_MX), wo_ref[...], preferred_element_type=F32)
    o_ref[...] = x_ref[...] + _rms(mix, g_ref[...])


def _merge(x2, y_rnn, y_att, gates, w_rnn_out, w_att_out, w_o, g):
    n = x2.shape[0]
    row = lambda w: pl.BlockSpec((TM, w), lambda i: (i, 0))
    return pl.pallas_call(
        _merge_kernel,
        grid=(n // TM,),
        in_specs=[row(D_MODEL), row(D_RNN), row(D_Q), row(W_GATE),
                  _const_spec((D_RNN, D_MODEL)), _const_spec((D_Q, D_MODEL)),
                  _const_spec((D_MODEL, D_MODEL)), _const_spec((1, D_MODEL))],
        out_specs=row(D_MODEL),
        out_shape=jax.ShapeDtypeStruct((n, D_MODEL), F32),
        compiler_params=pltpu.CompilerParams(
            dimension_semantics=("arbitrary",), vmem_limit_bytes=VMEM_LIMIT),
        name="merge",
    )(x2, y_rnn, y_att, gates, w_rnn_out, w_att_out, w_o, g)


FF_CHUNK = D_FF // 2


def _ffn_kernel(x_ref, gpre_ref, wg_ref, wu_ref, wd_ref, gpost_ref, o_ref):
    x = x_ref[...]
    h = _rms(x, gpre_ref[...]).astype(_MX)
    f = jnp.zeros(x.shape, F32)
    for c in range(0, D_FF, FF_CHUNK):
        gate = jnp.dot(h, wg_ref[:, c:c + FF_CHUNK], preferred_element_type=F32)
        up = jnp.dot(h, wu_ref[:, c:c + FF_CHUNK], preferred_element_type=F32)
        act = (jax.nn.silu(gate) * up).astype(_MX)
        f = f + jnp.dot(act, wd_ref[c:c + FF_CHUNK, :], preferred_element_type=F32)
    o_ref[...] = x + _rms(f, gpost_ref[...])


def _ffn(x1, g_pre, w_gate, w_up, w_down, g_post):
    n = x1.shape[0]
    row = pl.BlockSpec((TM, D_MODEL), lambda i: (i, 0))
    return pl.pallas_call(
        _ffn_kernel,
        grid=(n // TM,),
        in_specs=[row, _const_spec((1, D_MODEL)),
                  _const_spec((D_MODEL, D_FF)), _const_spec((D_MODEL, D_FF)),
                  _const_spec((D_FF, D_MODEL)), _const_spec((1, D_MODEL))],
        out_specs=row,
        out_shape=jax.ShapeDtypeStruct((n, D_MODEL), F32),
        compiler_params=pltpu.CompilerParams(
            dimension_semantics=("arbitrary",), vmem_limit_bytes=VMEM_LIMIT),
        name="ffn",
    )(x1, g_pre, w_gate, w_up, w_down, g_post)


def _rope_tables(seq):
    half = ROT_DIM // 2
    inv_freq = ROPE_THETA ** (-jnp.arange(half, dtype=F32) / half)
    ang = jnp.arange(seq, dtype=F32)[:, None] * inv_freq[None, :]
    cos, sin = jnp.cos(ang), jnp.sin(ang)
    rest = HEAD_DIM - ROT_DIM
    ones, zeros, zh = jnp.ones((seq, rest), F32), jnp.zeros((seq, rest), F32), jnp.zeros((seq, half), F32)
    cos_t = jnp.concatenate([cos, cos, ones], axis=1)
    slo_t = jnp.concatenate([-sin, zh, zeros], axis=1)
    shi_t = jnp.concatenate([zh, sin, zeros], axis=1)
    two = lambda t: jnp.concatenate([t, t], axis=1)
    return two(cos_t), two(slo_t), two(shi_t)


def _gate_weights(w_a, w_x):
    per = GATE_GROUP // RNN_BLOCK

    def diag(w):
        w = w.reshape(D_RNN // GATE_GROUP, per, RNN_BLOCK, RNN_BLOCK)
        eye = jnp.eye(per, dtype=w.dtype)
        return jnp.einsum('gpij,pq->gpiqj', w, eye).reshape(-1, GATE_GROUP, GATE_GROUP)

    return jnp.concatenate([diag(w_a), diag(w_x)], axis=-1).astype(_MX)


def kernel(x, norm_mix_pre, w_in, conv_w, conv_b, rg_w_a, rg_b_a, rg_w_x, rg_b_x, rg_lambda,
           idx_k_ln_g, idx_k_ln_b, w_rnn_out, w_att_out, w_o, norm_mix_post,
           norm_ffn_pre, w_ffn_gate, w_ffn_up, w_ffn_down, norm_ffn_post):
    bsz, seq, _ = x.shape
    depth = w_in.shape[0]
    x2 = x.reshape(bsz * seq, D_MODEL)
    cos, slo, shi = _rope_tables(seq)
    o_idx = W_RNN + W_QKV
    for l in range(depth):
        w = w_in[l]
        w_all = jnp.concatenate(
            [w[:, :o_idx], w[:, o_idx + D_IDX:], w[:, o_idx:o_idx + D_IDX],
             jnp.zeros((D_MODEL, D_IDX_PAD - D_IDX), w.dtype)], axis=1).astype(_MX)
        pad_lanes = lambda t: jnp.pad(t, (0, LANES - IDX_DIM))[None]
        y_rnn, gates, q, k, vt, iq, ik, iwt = _front(
            x2, norm_mix_pre[l][None], w_all, conv_w[l], conv_b[l][None],
            _gate_weights(rg_w_a[l], rg_w_x[l]), rg_b_a[l][None], rg_b_x[l][None], rg_lambda[l][None],
            cos, slo, shi, pad_lanes(idx_k_ln_g[l]), pad_lanes(idx_k_ln_b[l]), bsz, seq)
        y_att = _attn(iq, ik, iwt, q, k, vt, bsz, seq)
        x1 = _merge(x2, y_rnn, y_att, gates, w_rnn_out[l].astype(_MX), w_att_out[l].astype(_MX),
                    w_o[l].astype(_MX), norm_mix_post[l][None])
        x2 = _ffn(x1, norm_ffn_pre[l][None], w_ffn_gate[l].astype(_MX), w_ffn_up[l].astype(_MX),
                  w_ffn_down[l].astype(_MX), norm_ffn_post[l][None])
    return x2.reshape(bsz, seq, D_MODEL)
```

```python
import functools
import math

import jax
import jax.numpy as jnp
from jax import lax
from jax.experimental import pallas as pl
from jax.experimental.pallas import tpu as pltpu

F32 = jnp.float32
I32 = jnp.int32
_MX = jnp.bfloat16

D_MODEL = 1024
D_RNN = 1024
N_RNN_BLOCKS = 16
RNN_BLOCK = D_RNN // N_RNN_BLOCKS
CONV_WIDTH = 4
LRU_C = 8.0
N_HEADS = 16
N_KV_HEADS = 4
HEAD_DIM = 64
ROT_DIM = HEAD_DIM // 4
ROPE_THETA = 500000.0
N_IDX_HEADS = 8
IDX_DIM = 64
TOPK_MAX = 256
D_FF = 2816
EPS = 1e-6

D_Q = N_HEADS * HEAD_DIM
D_KV = N_KV_HEADS * HEAD_DIM
D_QKV = D_Q + 2 * D_KV
D_IQ = N_IDX_HEADS * IDX_DIM
D_IDX = D_IQ + IDX_DIM + N_IDX_HEADS
D_IDX_PAD = 640
W_RNN, W_QKV, W_GATE = 2 * D_RNN, D_QKV, 2 * D_MODEL
W_ALL = W_RNN + W_QKV + W_GATE + D_IDX_PAD

LANES = 128
SUBLANES = 8
VMEM_LIMIT = 56 * 1024 * 1024
NEG_BIG = -1e30
INT_MIN = -(2 ** 31)

TM = 512
TS = 512
TQ = 256
CK = 512
GATE_GROUP = 256


def _const_spec(shape):
    nd = len(shape)
    return pl.BlockSpec(shape, lambda *_: (0,) * nd, pipeline_mode=pl.Buffered(1))


def _rms(x, g):
    return x * lax.rsqrt(jnp.mean(x * x, axis=-1, keepdims=True) + EPS) * g


def _sigmoid(x):
    return 0.5 * jnp.tanh(0.5 * x) + 0.5


def _project(h, w_ref, ref, col, width):
    step = 512 if width % 512 == 0 else width
    for c in range(0, width, step):
        ref[:, c:c + step] = jnp.dot(
            h, w_ref[:, col + c:col + c + step], preferred_element_type=F32).astype(ref.dtype)


def _rnn_carry_in(s, xpad_ref, hc_ref):
    ts = xpad_ref.shape[0] - SUBLANES

    @pl.when(s == 0)
    def _():
        xpad_ref[0:SUBLANES, :] = jnp.zeros((SUBLANES, D_RNN), F32)
        hc_ref[...] = jnp.zeros_like(hc_ref)

    @pl.when(s != 0)
    def _():
        xpad_ref[0:SUBLANES, :] = xpad_ref[ts:ts + SUBLANES, :]


def _rnn_gates(rnn_ref, cw_ref, cb_ref, wg_ref, ba_ref, bx_ref, lam_ref, xpad_ref, a_ref, h_ref, alongside):
    ts = rnn_ref.shape[0]
    xpad_ref[SUBLANES:SUBLANES + ts, :] = rnn_ref[:, 0:D_RNN].astype(F32)

    nl = -lam_ref[...]
    softplus_nl = jnp.maximum(nl, 0.0) + jnp.log1p(jnp.exp(-jnp.abs(nl)))

    for c in range(0, D_RNN, GATE_GROUP):
        alongside[c // GATE_GROUP]()
        cs = slice(c, c + GATE_GROUP)
        xc = cb_ref[:, cs]
        for k in range(CONV_WIDTH):
            off = SUBLANES - (CONV_WIDTH - 1) + k
            xc = xc + xpad_ref[off:off + ts, cs] * cw_ref[k:k + 1, cs]
        gz = jnp.dot(xc.astype(_MX), wg_ref[c // GATE_GROUP], preferred_element_type=F32)
        r = _sigmoid(gz[:, :GATE_GROUP] + ba_ref[:, cs])
        i = _sigmoid(gz[:, GATE_GROUP:] + bx_ref[:, cs])
        log_a = (-LRU_C * r) * softplus_nl[:, cs]
        a = jnp.exp(log_a)
        u = jnp.sqrt(-jnp.tanh(log_a) * (a * a + 1.0)) * (i * xc)
        for j in range(GATE_GROUP // LANES):
            a_ref[c // LANES + j] = a[:, j * LANES:(j + 1) * LANES]
            h_ref[c // LANES + j] = u[:, j * LANES:(j + 1) * LANES]


def _rnn_scan(a_ref, h_ref, sp_ref, sh_ref, cin_ref, hc_ref):
    ts = h_ref.shape[1]
    ng = ts // SUBLANES

    for j in range(D_RNN // LANES):
        ls = slice(j * LANES, (j + 1) * LANES)
        h = h_ref[j, pl.ds(0, ng, stride=SUBLANES), :]
        p = a_ref[j, pl.ds(0, ng, stride=SUBLANES), :]
        for i in range(1, SUBLANES):
            a_i = a_ref[j, pl.ds(i, ng, stride=SUBLANES), :]
            h = a_i * h + h_ref[j, pl.ds(i, ng, stride=SUBLANES), :]
            p = a_i * p
            h_ref[j, pl.ds(i, ng, stride=SUBLANES), :] = h
            a_ref[j, pl.ds(i, ng, stride=SUBLANES), :] = p
        sp_ref[:, ls] = p
        sh_ref[:, ls] = h

    def carry_body(g, carry):
        cin_ref[pl.ds(g, 1), :] = carry
        return sp_ref[pl.ds(g, 1), :] * carry + sh_ref[pl.ds(g, 1), :]

    hc_ref[0:1, :] = lax.fori_loop(0, ng, carry_body, hc_ref[0:1, :], unroll=True)

    def out_body(g, _):
        r0 = pl.multiple_of(g * SUBLANES, SUBLANES)
        rows = pl.ds(r0, SUBLANES)
        cin = cin_ref[pl.ds(g, 1), :]
        for j in range(D_RNN // LANES):
            cj = jnp.broadcast_to(cin[:, j * LANES:(j + 1) * LANES], (SUBLANES, LANES))
            h_ref[j, rows, :] = h_ref[j, rows, :] + a_ref[j, rows, :] * cj
        return 0

    lax.fori_loop(0, ng, out_body, 0, unroll=True)


def _rnn_out(rnn_ref, h_ref, y_ref):
    for j in range(D_RNN // LANES):
        gr = rnn_ref[:, D_RNN + j * LANES:D_RNN + (j + 1) * LANES].astype(F32)
        y_ref[:, j * LANES:(j + 1) * LANES] = (
            h_ref[j] * jax.nn.gelu(gr, approximate=True)).astype(y_ref.dtype)


def _rope_block(x, cos, sin_lo, sin_hi):
    w = x.shape[1]
    half = ROT_DIM // 2
    return x * cos + pltpu.roll(x, w - half, 1) * sin_lo + pltpu.roll(x, half, 1) * sin_hi


def _prep(qkv_ref, idx_ref, cos_ref, slo_ref, shi_ref, lng_ref, lnb_ref,
          q_ref, k_ref, vt_ref, iq_ref, ik_ref, iwt_ref):
    cos, slo, shi = cos_ref[...], slo_ref[...], shi_ref[...]
    tm = qkv_ref.shape[0]
    tq = q_ref.shape[2]
    scale = (HEAD_DIM ** -0.5) * math.log2(math.e)
    heads_per_blk = LANES // HEAD_DIM

    def put_heads(ref, blk, rb):
        for j in range(heads_per_blk):
            for t in range(tm // tq):
                ref[t, blk * heads_per_blk + j] = rb[t * tq:(t + 1) * tq, j * HEAD_DIM:(j + 1) * HEAD_DIM]

    for blk in range(D_Q // LANES):
        xb = qkv_ref[:, blk * LANES:(blk + 1) * LANES].astype(F32) * scale
        put_heads(q_ref, blk, _rope_block(xb, cos, slo, shi).astype(q_ref.dtype))
    ones_col = (lax.broadcasted_iota(I32, (tm, HEAD_DIM), 1) == 0).astype(F32)
    for blk in range(D_KV // LANES):
        c0 = D_Q + blk * LANES
        rb = _rope_block(qkv_ref[:, c0:c0 + LANES].astype(F32), cos, slo, shi).astype(k_ref.dtype)
        for j in range(heads_per_blk):
            k_ref[blk * heads_per_blk + j] = rb[:, j * HEAD_DIM:(j + 1) * HEAD_DIM]
            v0 = D_Q + D_KV + blk * LANES + j * HEAD_DIM
            v_aug = jnp.concatenate([qkv_ref[:, v0:v0 + HEAD_DIM].astype(F32), ones_col], axis=1)
            vt_ref[blk * heads_per_blk + j] = v_aug.T.astype(vt_ref.dtype)
    for blk in range(D_IQ // LANES):
        rb = _rope_block(idx_ref[:, blk * LANES:(blk + 1) * LANES], cos, slo, shi).astype(iq_ref.dtype)
        put_heads(iq_ref, blk, rb)
    tail = idx_ref[:, D_IQ:D_IQ + LANES]
    is_key = lax.broadcasted_iota(I32, (1, LANES), 1) < IDX_DIM
    mu = jnp.sum(jnp.where(is_key, tail, 0.0), axis=-1, keepdims=True) * (1.0 / IDX_DIM)
    cen = jnp.where(is_key, tail - mu, 0.0)
    var = jnp.sum(cen * cen, axis=-1, keepdims=True) * (1.0 / IDX_DIM)
    ik = cen * lax.rsqrt(var + EPS) * lng_ref[...] + lnb_ref[...]
    ik = _rope_block(ik, cos, slo, shi)
    ik_ref[...] = ik[:, :IDX_DIM].astype(ik_ref.dtype)
    iwt_ref[...] = tail.T[IDX_DIM:IDX_DIM + N_IDX_HEADS, :] * ((N_IDX_HEADS ** -0.5) * (IDX_DIM ** -0.5))


COL_RNN, COL_QKV, COL_GATE, COL_IDX = 0, W_RNN, W_RNN + W_QKV, W_RNN + W_QKV + W_GATE


def _front_kernel(x_ref, g_ref, w_ref, cw_ref, cb_ref, wg_ref, ba_ref, bx_ref, lam_ref,
                  cos_ref, slo_ref, shi_ref, lng_ref, lnb_ref,
                  y_ref, gate_ref, q_ref, k_ref, vt_ref, iq_ref, ik_ref, iwt_ref,
                  rnn_s, qkv_s, idx_s, xpad_ref, a_ref, h_ref, sp_ref, sh_ref, cin_ref, hc_ref, *, ns):
    _rnn_carry_in(pl.program_id(0) % ns, xpad_ref, hc_ref)
    h = _rms(x_ref[...], g_ref[...]).astype(_MX)
    _project(h, w_ref, rnn_s, COL_RNN, W_RNN)
    qkv_part = W_QKV // 3
    alongside = [functools.partial(_project, h, w_ref, qkv_s.at[:, i * qkv_part:(i + 1) * qkv_part],
                                   COL_QKV + i * qkv_part, qkv_part) for i in range(3)]
    alongside.append(functools.partial(_project, h, w_ref, idx_s, COL_IDX, D_IDX_PAD))
    _rnn_gates(rnn_s, cw_ref, cb_ref, wg_ref, ba_ref, bx_ref, lam_ref, xpad_ref, a_ref, h_ref, alongside)
    _rnn_scan(a_ref, h_ref, sp_ref, sh_ref, cin_ref, hc_ref)
    _project(h, w_ref, gate_ref, COL_GATE, W_GATE)
    _rnn_out(rnn_s, h_ref, y_ref)
    _prep(qkv_s, idx_s, cos_ref, slo_ref, shi_ref, lng_ref, lnb_ref,
          q_ref, k_ref, vt_ref, iq_ref, ik_ref, iwt_ref)


def _front(x2, g, w_all, conv_w, conv_b, w_gates, b_a, b_x, lam, cos, slo, shi, ln_g, ln_b, bsz, seq):
    n = x2.shape[0]
    tm = min(TM, seq)
    tq = min(TQ, seq)
    ns = seq // tm
    ng = tm // SUBLANES
    tok = lambda w: pl.BlockSpec((tm, w), lambda i: (i, 0))
    tab = pl.BlockSpec((tm, LANES), lambda i: (i % ns, 0))
    tiled = lambda nh: pl.BlockSpec((None, tm // tq, nh, tq, HEAD_DIM), lambda i: (i // ns, i % ns, 0, 0, 0))
    return pl.pallas_call(
        functools.partial(_front_kernel, ns=ns),
        grid=(n // tm,),
        in_specs=[tok(D_MODEL), _const_spec((1, D_MODEL)), _const_spec((D_MODEL, W_ALL)),
                  _const_spec((CONV_WIDTH, D_RNN)), _const_spec((1, D_RNN)),
                  _const_spec((D_RNN // GATE_GROUP, GATE_GROUP, 2 * GATE_GROUP)),
                  _const_spec((1, D_RNN)), _const_spec((1, D_RNN)), _const_spec((1, D_RNN)),
                  tab, tab, tab, _const_spec((1, LANES)), _const_spec((1, LANES))],
        out_specs=[tok(D_RNN), tok(W_GATE),
                   tiled(N_HEADS),
                   pl.BlockSpec((None, N_KV_HEADS, tm, HEAD_DIM), lambda i: (i // ns, 0, i % ns, 0)),
                   pl.BlockSpec((None, N_KV_HEADS, LANES, tm), lambda i: (i // ns, 0, 0, i % ns)),
                   tiled(N_IDX_HEADS),
                   pl.BlockSpec((None, tm, IDX_DIM), lambda i: (i // ns, i % ns, 0)),
                   pl.BlockSpec((None, N_IDX_HEADS, tm), lambda i: (i // ns, 0, i % ns))],
        out_shape=[jax.ShapeDtypeStruct((n, D_RNN), _MX),
                   jax.ShapeDtypeStruct((n, W_GATE), _MX),
                   jax.ShapeDtypeStruct((bsz, seq // tq, N_HEADS, tq, HEAD_DIM), _MX),
                   jax.ShapeDtypeStruct((bsz, N_KV_HEADS, seq, HEAD_DIM), _MX),
                   jax.ShapeDtypeStruct((bsz, N_KV_HEADS, LANES, seq), _MX),
                   jax.ShapeDtypeStruct((bsz, seq // tq, N_IDX_HEADS, tq, IDX_DIM), _MX),
                   jax.ShapeDtypeStruct((bsz, seq, IDX_DIM), _MX),
                   jax.ShapeDtypeStruct((bsz, N_IDX_HEADS, seq), F32)],
        scratch_shapes=[pltpu.VMEM((tm, W_RNN), _MX),
                        pltpu.VMEM((tm, W_QKV), _MX),
                        pltpu.VMEM((tm, D_IDX_PAD), F32),
                        pltpu.VMEM((tm + SUBLANES, D_RNN), F32),
                        pltpu.VMEM((D_RNN // LANES, tm, LANES), F32),
                        pltpu.VMEM((D_RNN // LANES, tm, LANES), F32),
                        pltpu.VMEM((ng, D_RNN), F32),
                        pltpu.VMEM((ng, D_RNN), F32),
                        pltpu.VMEM((ng, D_RNN), F32),
                        pltpu.VMEM((SUBLANES, D_RNN), F32)],
        compiler_params=pltpu.CompilerParams(
            dimension_semantics=("arbitrary",), vmem_limit_bytes=VMEM_LIMIT),
        name="front",
    )(x2, g, w_all, conv_w, conv_b, w_gates, b_a, b_x, lam, cos, slo, shi, ln_g, ln_b)


_NT = (((1,), (1,)), ((), ()))
REP = N_HEADS // N_KV_HEADS
I16 = jnp.int16
HALF_BITS = 16
HALF_SIGN = 1 << (HALF_BITS - 1)
INT16_MIN = -HALF_SIGN
COUNT_ROWS = 64
IDX_HEADS_PER_DOT = 2


def _attn_kernel(iq_ref, ik_ref, iwt_ref, q_ref, k_ref, vt_ref, o_ref,
                 key_ref, hi_ref, lo_ref, bias_ref, m_ref, acc_ref, jmax_ref, *, topk, ck):
    qt = pl.program_id(1)
    tq = o_ref.shape[0]
    seq = ik_ref.shape[0]
    assert ck in (tq, 2 * tq)
    n_keys = qt * tq + tq
    n_full = n_keys // ck
    qpos = qt * tq + lax.broadcasted_iota(I32, (1, tq), 1)
    iwt = iwt_ref[...]

    def over_chunks(body, init=0):
        carry = lax.fori_loop(0, n_full, lambda c, x: body(pl.multiple_of(c * ck, ck), ck, x), init)
        if ck > tq:
            first = pl.multiple_of(n_full * ck, tq)
            carry = lax.cond(n_keys % ck != 0, lambda x: body(first, tq, x), lambda x: x, carry)
        return carry

    def key_pos(k0, rows):
        return k0 + lax.broadcasted_iota(I32, (rows, 1), 0)

    def fold_rows(x, op):
        y = op(x.reshape(x.shape[0] // COUNT_ROWS, COUNT_ROWS, x.shape[1]), axis=0).astype(F32)
        y = op(y.reshape(COUNT_ROWS // SUBLANES, SUBLANES, x.shape[1]), axis=0)
        return op(y, axis=0, keepdims=True)

    def score_body(k0, rows, _):
        ikc = ik_ref[pl.ds(k0, rows), :]
        acc = jnp.zeros((rows, tq), F32)
        for h0 in range(0, N_IDX_HEADS, IDX_HEADS_PER_DOT):
            iq = iq_ref[h0:h0 + IDX_HEADS_PER_DOT].reshape(IDX_HEADS_PER_DOT * tq, IDX_DIM)
            logit = lax.dot_general(ikc, iq, _NT, preferred_element_type=F32)
            for j in range(IDX_HEADS_PER_DOT):
                acc = acc + jnp.maximum(logit[:, j * tq:(j + 1) * tq], 0.0) * iwt[h0 + j:h0 + j + 1, :]
        acc = jnp.where(acc == 0.0, 0.0, acc)
        bits = pltpu.bitcast(acc, I32)
        key = bits ^ ((bits >> 31) & 0x7FFFFFFF)
        key = jnp.where(key_pos(k0, rows) <= qpos, key, INT_MIN)
        key_ref[pl.ds(k0, rows), :] = key
        hi_ref[pl.ds(k0, rows), :] = (key >> HALF_BITS).astype(I16)
        return 0

    over_chunks(score_body)

    def count_over(ref, pred, one, zero, width):
        def body(k0, rows, acc):
            hit = jnp.where(pred(ref[pl.ds(k0, rows), :], key_pos(k0, rows)), one, zero)
            for j in range(rows // COUNT_ROWS):
                acc = acc + hit[j * COUNT_ROWS:(j + 1) * COUNT_ROWS]
            return acc
        acc = over_chunks(body, jnp.zeros((COUNT_ROWS, tq), width))
        return jnp.sum(acc.astype(F32), axis=0, keepdims=True)

    def count(pred):
        return count_over(key_ref, pred, 1.0, 0.0, F32)

    def count_half(ref, pred):
        return count_over(ref, pred, jnp.int16(1), jnp.int16(0), I16)

    def kth_largest_half(ref, k_target, n_start):
        def bit_body(i, carry):
            thr, n_ge = carry
            cand = thr + lax.shift_left(jnp.int32(1), HALF_BITS - 1 - i)
            cand16 = cand.astype(I16)
            cnt = count_half(ref, lambda v, kpos: v >= cand16)
            take = cnt >= k_target
            return jnp.where(take, cand, thr), jnp.where(take, cnt, n_ge)
        return lax.fori_loop(0, HALF_BITS, bit_body, (jnp.full((1, tq), INT16_MIN, I32), n_start))

    kk = jnp.minimum(qpos + 1, topk).astype(F32)
    thr_hi, n_ge_hi = kth_largest_half(hi_ref, kk, (qpos + 1).astype(F32))
    thr_hi16 = thr_hi.astype(I16)
    n_gt_hi = count_half(hi_ref, lambda v, kpos: v > thr_hi16)

    def low_body(k0, rows, _):
        low = (key_ref[pl.ds(k0, rows), :] ^ HALF_SIGN).astype(I16)
        lo_ref[pl.ds(k0, rows), :] = jnp.where(
            hi_ref[pl.ds(k0, rows), :] == thr_hi16, low, jnp.int16(INT16_MIN))
        return 0

    over_chunks(low_body)
    thr_lo, n_ge_lo = kth_largest_half(lo_ref, kk - n_gt_hi, n_ge_hi - n_gt_hi)
    thr = thr_hi * (1 << HALF_BITS) + (thr_lo - INT16_MIN)
    n_ge = n_gt_hi + n_ge_lo

    jmax_ref[...] = jnp.full(jmax_ref.shape, seq, I32)

    @pl.when(jnp.max(n_ge - kk) > 0.0)
    def _():
        need = kk - count(lambda keys, kpos: keys > thr)
        nbits = max(1, (seq - 1).bit_length())

        def jbit_body(i, x):
            cand = x + lax.shift_left(jnp.int32(1), nbits - 1 - i)
            below = count(lambda keys, kpos: (keys == thr) & (kpos < cand))
            return jnp.where(below < need, cand, x)

        x = lax.fori_loop(0, nbits, jbit_body, jnp.zeros((1, tq), I32))
        jmax_ref[...] = jnp.broadcast_to(x, jmax_ref.shape)

    jmax = jmax_ref[0:1, :]

    def bias_body(k0, rows, _):
        keys = key_ref[pl.ds(k0, rows), :]
        kpos = key_pos(k0, rows)
        sel = ((keys > thr) | ((keys == thr) & (kpos <= jmax))) & (kpos <= qpos)
        bias_ref[pl.ds(k0, rows), :] = jnp.where(sel, 0.0, NEG_BIG).astype(bias_ref.dtype)
        return 0

    over_chunks(bias_body)

    m_ref[...] = jnp.full(m_ref.shape, NEG_BIG, F32)
    acc_ref[...] = jnp.zeros(acc_ref.shape, F32)

    def attn_body(k0, rows, _):
        bias = bias_ref[pl.ds(k0, rows), :]

        def logits(g):
            q = q_ref[g * REP:(g + 1) * REP].reshape(REP * tq, HEAD_DIM)
            return lax.dot_general(k_ref[g, pl.ds(k0, rows), :], q, _NT, preferred_element_type=F32)

        s_next = logits(0)
        for g in range(N_KV_HEADS):
            s = s_next.astype(_MX)
            if g + 1 < N_KV_HEADS:
                s_next = logits(g + 1)
            s = jnp.concatenate([s[:, r * tq:(r + 1) * tq] + bias for r in range(REP)], axis=1)
            m_old = m_ref[g]
            m_new = jnp.maximum(m_old, fold_rows(s, jnp.max))
            m_ref[g] = m_new
            p = jnp.exp2(s - m_new.astype(_MX))
            acc_ref[g] = jnp.exp2(m_old - m_new) * acc_ref[g] + jnp.dot(
                vt_ref[g, :, pl.ds(k0, rows)], p, preferred_element_type=F32)
        return 0

    over_chunks(attn_body)

    for g in range(N_KV_HEADS):
        acc = acc_ref[g]
        out = acc[:HEAD_DIM, :] / acc[HEAD_DIM:HEAD_DIM + 1, :]
        for r in range(0, REP, 2):
            pair = jnp.concatenate([out[:, r * tq:(r + 1) * tq], out[:, (r + 1) * tq:(r + 2) * tq]], axis=0)
            h = g * REP + r
            o_ref[:, h * HEAD_DIM:(h + 2) * HEAD_DIM] = pair.T.astype(o_ref.dtype)


def _attn(iq, ik, iwt, q, k, vt, bsz, seq):
    topk = min(TOPK_MAX, seq // 4)
    tq = min(TQ, seq)
    ck = min(CK, seq)
    nq = seq // tq
    tile = lambda nh: pl.BlockSpec((None, None, nh, tq, HEAD_DIM), lambda b, t: (b, t, 0, 0, 0))
    return pl.pallas_call(
        functools.partial(_attn_kernel, topk=topk, ck=ck),
        grid=(bsz, nq),
        in_specs=[tile(N_IDX_HEADS),
                  pl.BlockSpec((None, seq, IDX_DIM), lambda b, t: (b, 0, 0)),
                  pl.BlockSpec((None, N_IDX_HEADS, tq), lambda b, t: (b, 0, t)),
                  tile(N_HEADS),
                  pl.BlockSpec((None, N_KV_HEADS, seq, HEAD_DIM), lambda b, t: (b, 0, 0, 0)),
                  pl.BlockSpec((None, N_KV_HEADS, LANES, seq), lambda b, t: (b, 0, 0, 0))],
        out_specs=pl.BlockSpec((tq, D_Q), lambda b, t: (b * nq + t, 0)),
        out_shape=jax.ShapeDtypeStruct((bsz * seq, D_Q), _MX),
        scratch_shapes=[pltpu.VMEM((seq, tq), I32),
                        pltpu.VMEM((seq, tq), I16),
                        pltpu.VMEM((seq, tq), I16),
                        pltpu.VMEM((seq, tq), _MX),
                        pltpu.VMEM((N_KV_HEADS, 1, REP * tq), F32),
                        pltpu.VMEM((N_KV_HEADS, LANES, REP * tq), F32),
                        pltpu.VMEM((SUBLANES, tq), I32)],
        compiler_params=pltpu.CompilerParams(
            dimension_semantics=("arbitrary", "arbitrary"), vmem_limit_bytes=VMEM_LIMIT),
        name="attn",
    )(iq, ik, iwt, q, k, vt)


def _merge_kernel(x_ref, yr_ref, ya_ref, gate_ref, wr_ref, wa_ref, wo_ref, g_ref, o_ref):
    y_a = jnp.dot(yr_ref[...], wr_ref[...], preferred_element_type=F32)
    y_b = jnp.dot(ya_ref[...], wa_ref[...], preferred_element_type=F32)
    g_a = jax.nn.sigmoid(gate_ref[:, :D_MODEL].astype(F32))
    g_b = jax.nn.sigmoid(gate_ref[:, D_MODEL:].astype(F32))
    merged = g_a * y_a + g_b * y_b
    mix = jnp.dot(merged.astype(_MX), wo_ref[...], preferred_element_type=F32)
    o_ref[...] = x_ref[...] + _rms(mix, g_ref[...])


def _merge(x2, y_rnn, y_att, gates, w_rnn_out, w_att_out, w_o, g):
    n = x2.shape[0]
    row = lambda w: pl.BlockSpec((TM, w), lambda i: (i, 0))
    return pl.pallas_call(
        _merge_kernel,
        grid=(n // TM,),
        in_specs=[row(D_MODEL), row(D_RNN), row(D_Q), row(W_GATE),
                  _const_spec((D_RNN, D_MODEL)), _const_spec((D_Q, D_MODEL)),
                  _const_spec((D_MODEL, D_MODEL)), _const_spec((1, D_MODEL))],
        out_specs=row(D_MODEL),
        out_shape=jax.ShapeDtypeStruct((n, D_MODEL), F32),
        compiler_params=pltpu.CompilerParams(
            dimension_semantics=("arbitrary",), vmem_limit_bytes=VMEM_LIMIT),
        name="merge",
    )(x2, y_rnn, y_att, gates, w_rnn_out, w_att_out, w_o, g)


FF_CHUNK = D_FF // 2


def _ffn_kernel(x_ref, gpre_ref, wg_ref, wu_ref, wd_ref, gpost_ref, o_ref):
    x = x_ref[...]
    h = _rms(x, gpre_ref[...]).astype(_MX)
    f = jnp.zeros(x.shape, F32)
    for c in range(0, D_FF, FF_CHUNK):
        gate = jnp.dot(h, wg_ref[:, c:c + FF_CHUNK], preferred_element_type=F32)
        up = jnp.dot(h, wu_ref[:, c:c + FF_CHUNK], preferred_element_type=F32)
        act = (jax.nn.silu(gate) * up).astype(_MX)
        f = f + jnp.dot(act, wd_ref[c:c + FF_CHUNK, :], preferred_element_type=F32)
    o_ref[...] = x + _rms(f, gpost_ref[...])


def _ffn(x1, g_pre, w_gate, w_up, w_down, g_post):
    n = x1.shape[0]
    row = pl.BlockSpec((TM, D_MODEL), lambda i: (i, 0))
    return pl.pallas_call(
        _ffn_kernel,
        grid=(n // TM,),
        in_specs=[row, _const_spec((1, D_MODEL)),
                  _const_spec((D_MODEL, D_FF)), _const_spec((D_MODEL, D_FF)),
                  _const_spec((D_FF, D_MODEL)), _const_spec((1, D_MODEL))],
        out_specs=row,
        out_shape=jax.ShapeDtypeStruct((n, D_MODEL), F32),
        compiler_params=pltpu.CompilerParams(
            dimension_semantics=("arbitrary",), vmem_limit_bytes=VMEM_LIMIT),
        name="ffn",
    )(x1, g_pre, w_gate, w_up, w_down, g_post)


def _rope_tables(seq):
    half = ROT_DIM // 2
    inv_freq = ROPE_THETA ** (-jnp.arange(half, dtype=F32) / half)
    ang = jnp.arange(seq, dtype=F32)[:, None] * inv_freq[None, :]
    cos, sin = jnp.cos(ang), jnp.sin(ang)
    rest = HEAD_DIM - ROT_DIM
    ones, zeros, zh = jnp.ones((seq, rest), F32), jnp.zeros((seq, rest), F32), jnp.zeros((seq, half), F32)
    cos_t = jnp.concatenate([cos, cos, ones], axis=1)
    slo_t = jnp.concatenate([-sin, zh, zeros], axis=1)
    shi_t = jnp.concatenate([zh, sin, zeros], axis=1)
    two = lambda t: jnp.concatenate([t, t], axis=1)
    return two(cos_t), two(slo_t), two(shi_t)


def _gate_weights(w_a, w_x):
    per = GATE_GROUP // RNN_BLOCK

    def diag(w):
        w = w.reshape(D_RNN // GATE_GROUP, per, RNN_BLOCK, RNN_BLOCK)
        eye = jnp.eye(per, dtype=w.dtype)
        return jnp.einsum('gpij,pq->gpiqj', w, eye).reshape(-1, GATE_GROUP, GATE_GROUP)

    return jnp.concatenate([diag(w_a), diag(w_x)], axis=-1).astype(_MX)


def kernel(x, norm_mix_pre, w_in, conv_w, conv_b, rg_w_a, rg_b_a, rg_w_x, rg_b_x, rg_lambda,
           idx_k_ln_g, idx_k_ln_b, w_rnn_out, w_att_out, w_o, norm_mix_post,
           norm_ffn_pre, w_ffn_gate, w_ffn_up, w_ffn_down, norm_ffn_post):
    bsz, seq, _ = x.shape
    depth = w_in.shape[0]
    x2 = x.reshape(bsz * seq, D_MODEL)
    cos, slo, shi = _rope_tables(seq)
    o_idx = W_RNN + W_QKV
    for l in range(depth):
        w = w_in[l]
        w_all = jnp.concatenate(
            [w[:, :o_idx], w[:, o_idx + D_IDX:], w[:, o_idx:o_idx + D_IDX],
             jnp.zeros((D_MODEL, D_IDX_PAD - D_IDX), w.dtype)], axis=1).astype(_MX)
        pad_lanes = lambda t: jnp.pad(t, (0, LANES - IDX_DIM))[None]
        y_rnn, gates, q, k, vt, iq, ik, iwt = _front(
            x2, norm_mix_pre[l][None], w_all, conv_w[l], conv_b[l][None],
            _gate_weights(rg_w_a[l], rg_w_x[l]), rg_b_a[l][None], rg_b_x[l][None], rg_lambda[l][None],
            cos, slo, shi, pad_lanes(idx_k_ln_g[l]), pad_lanes(idx_k_ln_b[l]), bsz, seq)
        y_att = _attn(iq, ik, iwt, q, k, vt, bsz, seq)
        x1 = _merge(x2, y_rnn, y_att, gates, w_rnn_out[l].astype(_MX), w_att_out[l].astype(_MX),
                    w_o[l].astype(_MX), norm_mix_post[l][None])
        x2 = _ffn(x1, norm_ffn_pre[l][None], w_ffn_gate[l].astype(_MX), w_ffn_up[l].astype(_MX),
                  w_ffn_down[l].astype(_MX), norm_ffn_post[l][None])
    return x2.reshape(bsz, seq, D_MODEL)
```
